```python
import jax, jax.numpy as jnp
from jax import lax
import numpy as np

D_MODEL = 1024
BATCH = 2
SEQ = 8192
DEPTH = 4

GROUP_WIDTH = D_MODEL // 4
N_GROUP_HEADS = 4
HEAD_DIM = GROUP_WIDTH // N_GROUP_HEADS
SC_KERNEL = 3
CF_KERNEL = 31
FFN_KERNEL = 3
MOBA_BLOCK = 256
MOBA_TOPK = 3
QUERY_BLOCK = 128
ROPE_THETA = 500000.0
ROPE_DIM = HEAD_DIM // 4
MLA_Q_RANK = 3 * GROUP_WIDTH // 2
MLA_KV_RANK = GROUP_WIDTH // 2
MLA_NOPE_DIM = HEAD_DIM
MLA_ROPE_DIM = HEAD_DIM // 2
MLA_V_DIM = HEAD_DIM
D_FF = ((8 * D_MODEL // 3 + 127) // 128) * 128
EPS = 1e-6
NEG = -1e30

SC_COLS = 3 * GROUP_WIDTH
CF_COLS = 2 * GROUP_WIDTH
MOBA_COLS = 3 * GROUP_WIDTH
MLA_COLS = MLA_Q_RANK + MLA_KV_RANK + MLA_ROPE_DIM
IN_COLS = SC_COLS + CF_COLS + MOBA_COLS + MLA_COLS
SPLITS = (SC_COLS, SC_COLS + CF_COLS, SC_COLS + CF_COLS + MOBA_COLS)
MIX_WIDTH = 4 * GROUP_WIDTH

kernel_name = "hybrid_parallel_conv_moba_mla_block"


def rmsnorm(x, g):
    xf = x.astype(jnp.float32)
    y = xf * lax.rsqrt(jnp.mean(xf * xf, axis=-1, keepdims=True) + EPS)
    return (y * g.astype(jnp.float32)).astype(x.dtype)


def layernorm(x, g, b):
    xf = x.astype(jnp.float32)
    mu = jnp.mean(xf, axis=-1, keepdims=True)
    xc = xf - mu
    y = xc * lax.rsqrt(jnp.mean(xc * xc, axis=-1, keepdims=True) + EPS)
    return (y * g.astype(jnp.float32) + b.astype(jnp.float32)).astype(x.dtype)


def causal_dwconv(x, w):
    k, c = w.shape
    xp = jnp.pad(x, ((0, 0), (k - 1, 0), (0, 0)))
    return lax.conv_general_dilated(xp, w.astype(x.dtype)[:, None, :], (1,), 'VALID',
                                    dimension_numbers=('NWC', 'WIO', 'NWC'),
                                    feature_group_count=c)


def rope_tables(seq, dim):
    inv = 1.0 / (ROPE_THETA ** (jnp.arange(0, dim, 2, dtype=jnp.float32) / dim))
    ang = jnp.arange(seq, dtype=jnp.float32)[:, None] * inv[None, :]
    return jnp.cos(ang), jnp.sin(ang)


def apply_rope(x, cos, sin):
    half = x.shape[-1] // 2
    x1, x2 = x[..., :half], x[..., half:]
    c = cos.astype(x.dtype)
    s = sin.astype(x.dtype)
    return jnp.concatenate([x1 * c - x2 * s, x2 * c + x1 * s], axis=-1)


def merge_chunks(outs):
    nq, b, h, qb, dh = outs.shape
    return jnp.transpose(outs, (1, 0, 3, 2, 4)).reshape(b, nq * qb, h * dh)


def short_conv_mixer(z, w_conv):
    h, b_gate, c_gate = jnp.split(z, 3, axis=-1)
    return b_gate * causal_dwconv(c_gate * h, w_conv)


def conformer_conv_mixer(z, w_conv, b_conv, ln_g, ln_b):
    a, g = jnp.split(z, 2, axis=-1)
    u = a * jax.nn.sigmoid(g)
    u = causal_dwconv(u, w_conv) + b_conv.astype(u.dtype)
    return jax.nn.silu(layernorm(u, ln_g, ln_b))


def moba_attention(z, cos, sin):
    b, s, _ = z.shape
    h, dh = N_GROUP_HEADS, HEAD_DIM
    q, k, v = [t.reshape(b, s, h, dh).transpose(0, 2, 1, 3) for t in jnp.split(z, 3, axis=-1)]
    q = jnp.concatenate([apply_rope(q[..., :ROPE_DIM], cos, sin), q[..., ROPE_DIM:]], axis=-1)
    k = jnp.concatenate([apply_rope(k[..., :ROPE_DIM], cos, sin), k[..., ROPE_DIM:]], axis=-1)
    nb = -(-s // MOBA_BLOCK)
    pad = nb * MOBA_BLOCK - s
    kp = jnp.pad(k, ((0, 0), (0, 0), (0, pad), (0, 0)))
    vp = jnp.pad(v, ((0, 0), (0, 0), (0, pad), (0, 0)))
    kb = kp.reshape(b, h, nb, MOBA_BLOCK, dh)
    vb = vp.reshape(b, h, nb, MOBA_BLOCK, dh)
    kmean = jnp.mean(kb.astype(jnp.float32), axis=3)
    n_sel = min(MOBA_TOPK, nb)
    scale = dh ** -0.5
    bi = jnp.arange(b)[:, None, None, None]
    hi = jnp.arange(h)[None, :, None, None]
    blk_ids = jnp.arange(nb)

    def chunk(c):
        q0 = c * QUERY_BLOCK
        qc = lax.dynamic_slice_in_dim(q, q0, QUERY_BLOCK, axis=2)
        own = q0 // MOBA_BLOCK
        gate = jnp.einsum('bhqd,bhnd->bhqn', qc.astype(jnp.float32), kmean)
        gate = jnp.where(blk_ids < own, gate, -jnp.inf)
        _, idx = lax.top_k(gate, n_sel)
        valid = idx < own
        kg = kb[bi, hi, idx]
        vg = vb[bi, hi, idx]
        s_sel = jnp.einsum('bhqd,bhqnkd->bhqnk', qc, kg).astype(jnp.float32) * scale
        s_sel = jnp.where(valid[..., None], s_sel, NEG)
        k_own = lax.dynamic_slice_in_dim(kp, own * MOBA_BLOCK, MOBA_BLOCK, axis=2)
        v_own = lax.dynamic_slice_in_dim(vp, own * MOBA_BLOCK, MOBA_BLOCK, axis=2)
        s_own = jnp.einsum('bhqd,bhkd->bhqk', qc, k_own).astype(jnp.float32) * scale
        qpos = q0 + jnp.arange(QUERY_BLOCK)
        kpos = own * MOBA_BLOCK + jnp.arange(MOBA_BLOCK)
        s_own = jnp.where(kpos[None, :] <= qpos[:, None], s_own, NEG)
        sc = jnp.concatenate([s_sel.reshape(b, h, QUERY_BLOCK, n_sel * MOBA_BLOCK), s_own], axis=-1)
        p = jax.nn.softmax(sc, axis=-1).astype(v.dtype)
        p_sel = p[..., :n_sel * MOBA_BLOCK].reshape(b, h, QUERY_BLOCK, n_sel, MOBA_BLOCK)
        p_own = p[..., n_sel * MOBA_BLOCK:]
        return (jnp.einsum('bhqnk,bhqnkd->bhqd', p_sel, vg)
                + jnp.einsum('bhqk,bhkd->bhqd', p_own, v_own))

    return merge_chunks(lax.map(chunk, jnp.arange(s // QUERY_BLOCK)))


def mla_attention(z, g_q, g_kv, w_uq, w_ukv, cos, sin):
    b, s, _ = z.shape
    h = N_GROUP_HEADS
    cq, ckv, kr = jnp.split(z, (MLA_Q_RANK, MLA_Q_RANK + MLA_KV_RANK), axis=-1)
    q = (rmsnorm(cq, g_q) @ w_uq).reshape(b, s, h, MLA_NOPE_DIM + MLA_ROPE_DIM).transpose(0, 2, 1, 3)
    q_nope = q[..., :MLA_NOPE_DIM]
    q_rope = apply_rope(q[..., MLA_NOPE_DIM:], cos, sin)
    kv = (rmsnorm(ckv, g_kv) @ w_ukv).reshape(b, s, h, MLA_NOPE_DIM + MLA_V_DIM).transpose(0, 2, 1, 3)
    k_nope = kv[..., :MLA_NOPE_DIM]
    v = kv[..., MLA_NOPE_DIM:]
    k_rope = apply_rope(kr, cos, sin)
    scale = (MLA_NOPE_DIM + MLA_ROPE_DIM) ** -0.5
    kpos = jnp.arange(s)

    def chunk(c):
        q0 = c * QUERY_BLOCK
        qn = lax.dynamic_slice_in_dim(q_nope, q0, QUERY_BLOCK, axis=2)
        qr = lax.dynamic_slice_in_dim(q_rope, q0, QUERY_BLOCK, axis=2)
        sc = (jnp.einsum('bhqd,bhkd->bhqk', qn, k_nope)
              + jnp.einsum('bhqd,bkd->bhqk', qr, k_rope)).astype(jnp.float32) * scale
        qpos = q0 + jnp.arange(QUERY_BLOCK)
        sc = jnp.where(kpos[None, :] <= qpos[:, None], sc, NEG)
        p = jax.nn.softmax(sc, axis=-1).astype(v.dtype)
        return jnp.einsum('bhqk,bhkd->bhqd', p, v)

    return merge_chunks(lax.map(chunk, jnp.arange(s // QUERY_BLOCK)))


def conv_ffn_block(h, w_up, w_conv, w_down):
    u = causal_dwconv(h @ w_up, w_conv)
    g, v = jnp.split(u, 2, axis=-1)
    return (jax.nn.silu(g) * v) @ w_down


def setup_inputs(seed: int = 0) -> dict:
    key = jax.random.key(seed)
    ks = jax.random.split(key, 20)
    f32 = jnp.float32

    def nrm(k, shape, scale):
        return jax.random.normal(k, shape, f32) * scale

    def gain(k, shape):
        return 1.0 + 0.02 * jax.random.normal(k, shape, f32)

    L = DEPTH
    return {
        "x": jax.random.normal(ks[0], (BATCH, SEQ, D_MODEL), f32),
        "norm_mix_pre": gain(ks[1], (L, D_MODEL)),
        "norm_mix_post": gain(ks[2], (L, D_MODEL)),
        "norm_ffn_pre": gain(ks[3], (L, D_MODEL)),
        "norm_ffn_post": gain(ks[4], (L, D_MODEL)),
        "w_in": nrm(ks[5], (L, D_MODEL, IN_COLS), D_MODEL ** -0.5),
        "conv_a": nrm(ks[6], (L, SC_KERNEL, GROUP_WIDTH), SC_KERNEL ** -0.5),
        "conv_b": nrm(ks[7], (L, CF_KERNEL, GROUP_WIDTH), CF_KERNEL ** -0.5),
        "conv_b_bias": nrm(ks[8], (L, GROUP_WIDTH), 0.01),
        "ln_b_gain": gain(ks[9], (L, GROUP_WIDTH)),
        "ln_b_bias": nrm(ks[10], (L, GROUP_WIDTH), 0.01),
        "mla_q_norm": gain(ks[11], (L, MLA_Q_RANK)),
        "mla_kv_norm": gain(ks[12], (L, MLA_KV_RANK)),
        "w_uq": nrm(ks[13], (L, MLA_Q_RANK, N_GROUP_HEADS * (MLA_NOPE_DIM + MLA_ROPE_DIM)), MLA_Q_RANK ** -0.5),
        "w_ukv": nrm(ks[14], (L, MLA_KV_RANK, N_GROUP_HEADS * (MLA_NOPE_DIM + MLA_V_DIM)), MLA_KV_RANK ** -0.5),
        "w_out": nrm(ks[15], (L, MIX_WIDTH, D_MODEL), MIX_WIDTH ** -0.5),
        "w_ffn_up": nrm(ks[16], (L, D_MODEL, 2 * D_FF), D_MODEL ** -0.5),
        "conv_ffn": nrm(ks[17], (L, FFN_KERNEL, 2 * D_FF), FFN_KERNEL ** -0.5),
        "w_ffn_down": nrm(ks[18], (L, D_FF, D_MODEL), D_FF ** -0.5),
    }


def reference(x, norm_mix_pre, norm_mix_post, norm_ffn_pre, norm_ffn_post, w_in,
              conv_a, conv_b, conv_b_bias, ln_b_gain, ln_b_bias, mla_q_norm, mla_kv_norm,
              w_uq, w_ukv, w_out, w_ffn_up, conv_ffn, w_ffn_down):
    seq = x.shape[1]
    cos_c, sin_c = rope_tables(seq, ROPE_DIM)
    cos_d, sin_d = rope_tables(seq, MLA_ROPE_DIM)
    for l in range(DEPTH):
        h = rmsnorm(x, norm_mix_pre[l])
        z = h @ w_in[l]
        z_sc, z_cf, z_moba, z_mla = jnp.split(z, SPLITS, axis=-1)
        y_a = short_conv_mixer(z_sc, conv_a[l])
        y_b = conformer_conv_mixer(z_cf, conv_b[l], conv_b_bias[l], ln_b_gain[l], ln_b_bias[l])
        y_c = moba_attention(z_moba, cos_c, sin_c)
        y_d = mla_attention(z_mla, mla_q_norm[l], mla_kv_norm[l], w_uq[l], w_ukv[l], cos_d, sin_d)
        y = jnp.concatenate([y_a, y_b, y_c, y_d], axis=-1) @ w_out[l]
        x = x + rmsnorm(y, norm_mix_post[l])
        f = conv_ffn_block(rmsnorm(x, norm_ffn_pre[l]), w_ffn_up[l], conv_ffn[l], w_ffn_down[l])
        x = x + rmsnorm(f, norm_ffn_post[l])
    return x
```

```python
import functools

import jax
import jax.numpy as jnp
from jax import lax
from jax.experimental import pallas as pl
from jax.experimental.pallas import tpu as pltpu

F32 = jnp.float32
BF16 = jnp.bfloat16

D_MODEL = 1024
GROUP_WIDTH = 256
N_HEADS = 4
HEAD_DIM = 64
SC_KERNEL = 3
CF_KERNEL = 31
FFN_KERNEL = 3
MOBA_BLOCK = 256
MOBA_TOPK = 3
ROPE_THETA = 500000.0
ROPE_DIM = 16
MLA_Q_RANK = 384
MLA_KV_RANK = 128
MLA_NOPE_DIM = 64
MLA_ROPE_DIM = 32
D_FF = 2816
EPS = 1e-6
NEG = -1e30

LANES = 128
SUBLANES = 8
VMEM_LIMIT = 56 * 1024 * 1024

TOKEN_TILE = 512
ATTN_TILE = MOBA_BLOCK
CONV_ROWS = 64
CF_HALO = 32
FFN_CHUNK = 256

_C_SC = 0
_C_CF = _C_SC + 3 * GROUP_WIDTH
_C_MQ = _C_CF + 2 * GROUP_WIDTH
_C_MK = _C_MQ + N_HEADS * LANES
_C_MV = _C_MK + N_HEADS * LANES
_C_CQ = _C_MV + N_HEADS * LANES
_C_CKV = _C_CQ + MLA_Q_RANK
_C_KR = _C_CKV + MLA_KV_RANK
_C_END = _C_KR + LANES


def _const_spec(shape):
    zeros = (0,) * len(shape)
    return pl.BlockSpec(shape, lambda *_: zeros, pipeline_mode=pl.Buffered(1))


def _rms(x, g):
    return x * lax.rsqrt(jnp.mean(x * x, axis=-1, keepdims=True) + EPS) * g


def _sigmoid(x):
    return 1.0 / (1.0 + jnp.exp(-x))


def _rope(x, tab_ref, shift):
    return (x * tab_ref[0]
            + pltpu.roll(x, LANES - shift, 1) * tab_ref[1]
            + pltpu.roll(x, shift, 1) * tab_ref[2])


def _carry_halo(ext_ref, halo, rows, first):
    @pl.when(first)
    def _():
        ext_ref[0:halo, :] = jnp.zeros((halo, ext_ref.shape[1]), F32)

    @pl.when(jnp.logical_not(first))
    def _():
        ext_ref[0:halo, :] = ext_ref[rows:rows + halo, :]


def _mix_in_kernel(x_ref, g_ref, w_ref, wuq_ref, wukv_ref, gq_ref, gkv_ref,
                   ca_ref, cb_ref, cbb_ref, lng_ref, lnb_ref, tc_ref, td_ref, oh_ref,
                   ya_ref, yb_ref, qc_ref, kc_ref, vc_ref, qd_ref, kd_ref, vd_ref,
                   exta_ref, extb_ref):
    tm = x_ref.shape[1]
    first = pl.program_id(1) == 0
    h = _rms(x_ref[0], g_ref[...]).astype(BF16)

    def proj(a, b):
        return jnp.dot(h, w_ref[:, a:b], preferred_element_type=F32)

    hi_ones = (lax.broadcasted_iota(jnp.int32, (tm, LANES), 1) >= HEAD_DIM).astype(F32)

    z = proj(_C_SC, _C_CF)
    _carry_halo(exta_ref, SUBLANES, tm, first)
    exta_ref[SUBLANES:SUBLANES + tm, :] = z[:, 2 * GROUP_WIDTH:] * z[:, :GROUP_WIDTH]
    conv = ca_ref[0:1, :] * exta_ref[SUBLANES - 2:SUBLANES - 2 + tm, :]
    conv += ca_ref[1:2, :] * exta_ref[SUBLANES - 1:SUBLANES - 1 + tm, :]
    conv += ca_ref[2:3, :] * exta_ref[SUBLANES:SUBLANES + tm, :]
    ya_ref[0] = (z[:, GROUP_WIDTH:2 * GROUP_WIDTH] * conv).astype(BF16)

    z = proj(_C_CF, _C_MQ)
    _carry_halo(extb_ref, CF_HALO, tm, first)
    extb_ref[CF_HALO:CF_HALO + tm, :] = z[:, :GROUP_WIDTH] * _sigmoid(z[:, GROUP_WIDTH:])
    for r in range(tm // CONV_ROWS):
        base = r * CONV_ROWS + CF_HALO - (CF_KERNEL - 1)
        acc = jnp.broadcast_to(cbb_ref[...], (CONV_ROWS, GROUP_WIDTH))
        for k in range(CF_KERNEL):
            acc = acc + cb_ref[k:k + 1, :] * extb_ref[base + k:base + k + CONV_ROWS, :]
        xc = acc - jnp.mean(acc, axis=-1, keepdims=True)
        y = xc * lax.rsqrt(jnp.mean(xc * xc, axis=-1, keepdims=True) + EPS) * lng_ref[...] + lnb_ref[...]
        yb_ref[0, r * CONV_ROWS:(r + 1) * CONV_ROWS, :] = (y * _sigmoid(y)).astype(BF16)

    zq = proj(_C_MQ, _C_MK)
    zk = proj(_C_MK, _C_MV)
    zv = proj(_C_MV, _C_CQ)
    scale_c = HEAD_DIM ** -0.5
    for hd in range(N_HEADS):
        sl = slice(hd * LANES, (hd + 1) * LANES)
        qc_ref[0, hd] = (_rope(zq[:, sl], tc_ref, ROPE_DIM // 2) * scale_c).astype(BF16)
        kc_ref[0, hd] = (_rope(zk[:, sl], tc_ref, ROPE_DIM // 2) + oh_ref[...]).astype(BF16)
        vc_ref[0, hd] = (zv[:, sl] + hi_ones).astype(BF16)

    z = proj(_C_CQ, _C_END)
    cq = _rms(z[:, :MLA_Q_RANK], gq_ref[...]).astype(BF16)
    ckv = _rms(z[:, MLA_Q_RANK:MLA_Q_RANK + MLA_KV_RANK], gkv_ref[...]).astype(BF16)
    kr = _rope(z[:, MLA_Q_RANK + MLA_KV_RANK:], td_ref, MLA_ROPE_DIM // 2)
    qd = jnp.dot(cq, wuq_ref[...], preferred_element_type=F32)
    kvd = jnp.dot(ckv, wukv_ref[...], preferred_element_type=F32)
    scale_d = (MLA_NOPE_DIM + MLA_ROPE_DIM) ** -0.5
    for hd in range(N_HEADS):
        sl = slice(hd * LANES, (hd + 1) * LANES)
        slv = slice((N_HEADS + hd) * LANES, (N_HEADS + hd + 1) * LANES)
        qd_ref[0, hd] = (_rope(qd[:, sl], td_ref, MLA_ROPE_DIM // 2) * scale_d).astype(BF16)
        kd_ref[0, hd] = (kvd[:, sl] + kr).astype(BF16)
        vd_ref[0, hd] = (kvd[:, slv] + hi_ones).astype(BF16)


def _mix_in(x, g, w, wuq, wukv, gq, gkv, ca, cb, cbb, lng, lnb, tc, td, oh):
    b, s, _ = x.shape
    tm = TOKEN_TILE
    head_shape = jax.ShapeDtypeStruct((b, N_HEADS, s, LANES), BF16)
    head_spec = pl.BlockSpec((1, N_HEADS, tm, LANES), lambda bi, i: (bi, 0, i, 0))
    group_shape = jax.ShapeDtypeStruct((b, s, GROUP_WIDTH), BF16)
    group_spec = pl.BlockSpec((1, tm, GROUP_WIDTH), lambda bi, i: (bi, i, 0))
    tab_spec = pl.BlockSpec((3, tm, LANES), lambda bi, i: (0, i, 0))
    return pl.pallas_call(
        _mix_in_kernel,
        grid=(b, s // tm),
        in_specs=[
            pl.BlockSpec((1, tm, D_MODEL), lambda bi, i: (bi, i, 0)),
            _const_spec(g.shape), _const_spec(w.shape), _const_spec(wuq.shape), _const_spec(wukv.shape),
            _const_spec(gq.shape), _const_spec(gkv.shape), _const_spec(ca.shape), _const_spec(cb.shape),
            _const_spec(cbb.shape), _const_spec(lng.shape), _const_spec(lnb.shape),
            tab_spec, tab_spec, pl.BlockSpec((tm, LANES), lambda bi, i: (i, 0)),
        ],
        out_specs=[group_spec, group_spec] + [head_spec] * 6,
        out_shape=[group_shape, group_shape] + [head_shape] * 6,
        scratch_shapes=[pltpu.VMEM((tm + SUBLANES, GROUP_WIDTH), F32),
                        pltpu.VMEM((tm + CF_HALO, GROUP_WIDTH), F32)],
        compiler_params=pltpu.CompilerParams(
            dimension_semantics=("arbitrary", "arbitrary"), vmem_limit_bytes=VMEM_LIMIT),
        name="mix_in",
    )(x, g, w, wuq, wukv, gq, gkv, ca, cb, cbb, lng, lnb, tc, td, oh)


_NT = (((1,), (1,)), ((), ()))


def _block_means(k_ref, hd, gmat_ref):
    nblk = k_ref.shape[2] // MOBA_BLOCK
    keep = (lax.broadcasted_iota(jnp.int32, (1, LANES), 1) < HEAD_DIM).astype(F32)
    gmat_ref[hd] = jnp.zeros((LANES, LANES), F32)
    for n in range(nblk):
        blk = k_ref[0, hd, n * MOBA_BLOCK:(n + 1) * MOBA_BLOCK, :].astype(F32)
        gmat_ref[hd, HEAD_DIM + n:HEAD_DIM + n + 1, :] = jnp.mean(blk, axis=0, keepdims=True) * keep


def _select_bias(gate, own):
    lane = lax.broadcasted_iota(jnp.int32, gate.shape, 1)
    lane_f = lane.astype(F32)
    valid = jnp.logical_and(lane >= HEAD_DIM, lane < HEAD_DIM + own)
    g = jnp.where(valid, gate, -jnp.inf)
    keep = jnp.zeros(gate.shape, F32)
    for _ in range(MOBA_TOPK):
        top = jnp.max(g, axis=-1, keepdims=True)
        first = jnp.min(jnp.where(g == top, lane_f, float(2 * LANES)), axis=-1, keepdims=True)
        pick = lane_f == first
        keep = jnp.where(pick, 1.0, keep)
        g = jnp.where(pick, -jnp.inf, g)
    gate_lane = jnp.logical_and(lane >= HEAD_DIM, lane < HEAD_DIM + MOBA_BLOCK // SUBLANES)
    masked = jnp.logical_and(gate_lane, jnp.logical_not(jnp.logical_and(valid, keep > 0.5)))
    return jnp.where(masked, NEG, 0.0)


def _attn_kernel(q_ref, k_ref, v_ref, o_ref, *scratch, moba):
    tq = q_ref.shape[2]
    qi = pl.program_id(2)
    row = lax.broadcasted_iota(jnp.int32, (tq, MOBA_BLOCK), 0)
    col = lax.broadcasted_iota(jnp.int32, (tq, MOBA_BLOCK), 1)
    causal = col <= row
    lane = lax.broadcasted_iota(jnp.int32, (tq, LANES), 1)
    halves = []
    for hd in range(2):
        q = q_ref[0, hd]
        if moba:
            gmat_ref = scratch[0]

            @pl.when(qi == 0)
            def _():
                _block_means(k_ref, hd, gmat_ref)

            gate = lax.dot_general(q, gmat_ref[hd].astype(BF16), _NT, preferred_element_type=F32)
            q_past = (q.astype(F32) + _select_bias(gate, qi)).astype(BF16)
        else:
            q_past = q

        own = pl.multiple_of(qi * MOBA_BLOCK, MOBA_BLOCK)
        s = lax.dot_general(q, k_ref[0, hd, pl.ds(own, MOBA_BLOCK), :], _NT, preferred_element_type=F32)
        s = jnp.where(causal, s, NEG)
        m = jnp.max(s, axis=-1, keepdims=True)
        p = jnp.exp(s - m).astype(BF16)
        acc = jnp.dot(p, v_ref[0, hd, pl.ds(own, MOBA_BLOCK), :], preferred_element_type=F32)

        def past_block(j, carry, hd=hd, q_past=q_past):
            m, acc = carry
            start = pl.multiple_of(j * MOBA_BLOCK, MOBA_BLOCK)
            s = lax.dot_general(q_past, k_ref[0, hd, pl.ds(start, MOBA_BLOCK), :], _NT,
                                preferred_element_type=F32)
            m_new = jnp.maximum(m, jnp.max(s, axis=-1, keepdims=True))
            p = jnp.exp(s - m_new).astype(BF16)
            acc = jnp.exp(m - m_new) * acc + jnp.dot(
                p, v_ref[0, hd, pl.ds(start, MOBA_BLOCK), :], preferred_element_type=F32)
            return m_new, acc

        m, acc = lax.fori_loop(0, qi, past_block, (m, acc))
        swapped = pltpu.roll(acc, HEAD_DIM, 1)
        halves.append(acc / swapped if hd == 0 else swapped / acc)
    o_ref[0] = jnp.where(lane < HEAD_DIM, halves[0], halves[1]).astype(BF16)


def _attention(q, k, v, moba):
    b, nh, s, _ = q.shape
    tq = ATTN_TILE
    return pl.pallas_call(
        functools.partial(_attn_kernel, moba=moba),
        grid=(b, nh // 2, s // tq),
        in_specs=[
            pl.BlockSpec((1, 2, tq, LANES), lambda bi, hp, i: (bi, hp, i, 0)),
            pl.BlockSpec((1, 2, s, LANES), lambda bi, hp, i: (bi, hp, 0, 0)),
            pl.BlockSpec((1, 2, s, LANES), lambda bi, hp, i: (bi, hp, 0, 0)),
        ],
        out_specs=pl.BlockSpec((1, tq, LANES), lambda bi, hp, i: (bi, i, hp)),
        out_shape=jax.ShapeDtypeStruct((b, s, GROUP_WIDTH), BF16),
        scratch_shapes=[pltpu.VMEM((2, LANES, LANES), F32)] if moba else [],
        compiler_params=pltpu.CompilerParams(
            dimension_semantics=("arbitrary", "arbitrary", "arbitrary"), vmem_limit_bytes=VMEM_LIMIT),
        name="moba_attn" if moba else "mla_attn",
    )(q, k, v)


def _mix_out_kernel(x_ref, ya_ref, yb_ref, yc_ref, yd_ref, wo_ref, gpost_ref, gpre_ref, gfpost_ref,
                    wup_ref, cw_ref, wdn_ref, o_ref, ext_ref, carry_ref):
    tm = x_ref.shape[1]
    first = pl.program_id(1) == 0
    gw = GROUP_WIDTH
    y = jnp.dot(ya_ref[0], wo_ref[0:gw, :], preferred_element_type=F32)
    y += jnp.dot(yb_ref[0], wo_ref[gw:2 * gw, :], preferred_element_type=F32)
    y += jnp.dot(yc_ref[0], wo_ref[2 * gw:3 * gw, :], preferred_element_type=F32)
    y += jnp.dot(yd_ref[0], wo_ref[3 * gw:4 * gw, :], preferred_element_type=F32)
    x1 = x_ref[0] + _rms(y, gpost_ref[...])
    hn = _rms(x1, gpre_ref[...]).astype(BF16)

    @pl.when(first)
    def _():
        carry_ref[...] = jnp.zeros(carry_ref.shape, F32)

    fc = FFN_CHUNK
    f = jnp.zeros((tm, D_MODEL), F32)
    for c in range(D_FF // fc):
        convs = []
        for part in range(2):
            c0 = part * D_FF + c * fc
            ext = ext_ref.at[c % 2, part]
            ext[0:SUBLANES, :] = carry_ref[:, c0:c0 + fc]
            ext[SUBLANES:SUBLANES + tm, :] = jnp.dot(hn, wup_ref[:, c0:c0 + fc], preferred_element_type=F32)
            carry_ref[:, c0:c0 + fc] = ext[tm:tm + SUBLANES, :]
            conv = cw_ref[0:1, c0:c0 + fc] * ext[SUBLANES - 2:SUBLANES - 2 + tm, :]
            conv += cw_ref[1:2, c0:c0 + fc] * ext[SUBLANES - 1:SUBLANES - 1 + tm, :]
            conv += cw_ref[2:3, c0:c0 + fc] * ext[SUBLANES:SUBLANES + tm, :]
            convs.append(conv)
        act = (convs[0] * _sigmoid(convs[0]) * convs[1]).astype(BF16)
        f += jnp.dot(act, wdn_ref[c * fc:(c + 1) * fc, :], preferred_element_type=F32)
    o_ref[0] = x1 + _rms(f, gfpost_ref[...])


def _mix_out(x, ya, yb, yc, yd, wo, gpost, gpre, gfpost, wup, cw, wdn):
    b, s, _ = x.shape
    tm = TOKEN_TILE
    x_spec = pl.BlockSpec((1, tm, D_MODEL), lambda bi, i: (bi, i, 0))
    group_spec = pl.BlockSpec((1, tm, GROUP_WIDTH), lambda bi, i: (bi, i, 0))
    return pl.pallas_call(
        _mix_out_kernel,
        grid=(b, s // tm),
        in_specs=[x_spec, group_spec, group_spec, group_spec, group_spec,
                  _const_spec(wo.shape), _const_spec(gpost.shape), _const_spec(gpre.shape),
                  _const_spec(gfpost.shape), _const_spec(wup.shape), _const_spec(cw.shape),
                  _const_spec(wdn.shape)],
        out_specs=x_spec,
        out_shape=jax.ShapeDtypeStruct(x.shape, F32),
        scratch_shapes=[pltpu.VMEM((2, 2, tm + SUBLANES, FFN_CHUNK), F32),
                        pltpu.VMEM((SUBLANES, 2 * D_FF), F32)],
        compiler_params=pltpu.CompilerParams(
            dimension_semantics=("arbitrary", "arbitrary"), vmem_limit_bytes=VMEM_LIMIT),
        name="mix_out",
    )(x, ya, yb, yc, yd, wo, gpost, gpre, gfpost, wup, cw, wdn)


def _pad_heads(w, widths):
    per_head = sum(widths)
    w = w.reshape(w.shape[:-1] + (N_HEADS, per_head))
    w = jnp.pad(w, [(0, 0)] * (w.ndim - 1) + [(0, LANES - per_head)])
    return w.reshape(w.shape[:-2] + (N_HEADS * LANES,))


def _rope_tables(seq, dim, first_lane):
    half = dim // 2
    inv = 1.0 / (ROPE_THETA ** (jnp.arange(0, dim, 2, dtype=F32) / dim))
    ang = jnp.arange(seq, dtype=F32)[:, None] * inv[None, :]
    cos, sin = jnp.cos(ang), jnp.sin(ang)
    zero = jnp.zeros((seq, LANES), F32)
    c = jnp.ones((seq, LANES), F32)
    c = c.at[:, first_lane:first_lane + half].set(cos).at[:, first_lane + half:first_lane + dim].set(cos)
    s1 = zero.at[:, first_lane:first_lane + half].set(-sin)
    s2 = zero.at[:, first_lane + half:first_lane + dim].set(sin)
    return jnp.stack([c, s1, s2])


def _prep_w_in(w):
    sc, cf = w[:, :768], w[:, 768:1280]
    mq, mk, mv = w[:, 1280:1536], w[:, 1536:1792], w[:, 1792:2048]
    cq, ckv, kr = w[:, 2048:2432], w[:, 2432:2560], w[:, 2560:2592]
    kr = jnp.pad(kr, ((0, 0), (MLA_NOPE_DIM, LANES - MLA_NOPE_DIM - MLA_ROPE_DIM)))
    parts = [sc, cf, _pad_heads(mq, (HEAD_DIM,)), _pad_heads(mk, (HEAD_DIM,)), _pad_heads(mv, (HEAD_DIM,)),
             cq, ckv, kr]
    return jnp.concatenate(parts, axis=1).astype(BF16)


def _prep_w_ukv(w):
    w = w.reshape(MLA_KV_RANK, N_HEADS, 2, HEAD_DIM)
    k = _pad_heads(w[:, :, 0, :].reshape(MLA_KV_RANK, -1), (HEAD_DIM,))
    v = _pad_heads(w[:, :, 1, :].reshape(MLA_KV_RANK, -1), (HEAD_DIM,))
    return jnp.concatenate([k, v], axis=1).astype(BF16)


def kernel(x, norm_mix_pre, norm_mix_post, norm_ffn_pre, norm_ffn_post, w_in, conv_a, conv_b, conv_b_bias,
           ln_b_gain, ln_b_bias, mla_q_norm, mla_kv_norm, w_uq, w_ukv, w_out, w_ffn_up, conv_ffn, w_ffn_down):
    b, s, d = x.shape
    depth = w_in.shape[0]
    assert d == D_MODEL and s % TOKEN_TILE == 0 and s // MOBA_BLOCK <= MOBA_BLOCK // SUBLANES
    tc = _rope_tables(s, ROPE_DIM, 0)
    td = _rope_tables(s, MLA_ROPE_DIM, MLA_NOPE_DIM)
    blk = jnp.arange(s, dtype=jnp.int32)[:, None] // MOBA_BLOCK + HEAD_DIM
    oh = (blk == jnp.arange(LANES, dtype=jnp.int32)[None, :]).astype(F32)
    row = lambda v: v.reshape(1, -1)
    for l in range(depth):
        ya, yb, qc, kc, vc, qd, kd, vd = _mix_in(
            x, row(norm_mix_pre[l]), _prep_w_in(w_in[l]),
            _pad_heads(w_uq[l], (MLA_NOPE_DIM, MLA_ROPE_DIM)).astype(BF16), _prep_w_ukv(w_ukv[l]),
            row(mla_q_norm[l]), row(mla_kv_norm[l]), conv_a[l], conv_b[l], row(conv_b_bias[l]),
            row(ln_b_gain[l]), row(ln_b_bias[l]), tc, td, oh)
        yc = _attention(qc, kc, vc, True)
        yd = _attention(qd, kd, vd, False)
        x = _mix_out(x, ya, yb, yc, yd, w_out[l].astype(BF16), row(norm_mix_post[l]), row(norm_ffn_pre[l]),
                     row(norm_ffn_post[l]), w_ffn_up[l].astype(BF16), conv_ffn[l], w_ffn_down[l].astype(BF16))
    return x
```

```python
import functools

import jax
import jax.numpy as jnp
from jax import lax
from jax.experimental import pallas as pl
from jax.experimental.pallas import tpu as pltpu

F32 = jnp.float32
BF16 = jnp.bfloat16

D_MODEL = 1024
GROUP_WIDTH = 256
N_HEADS = 4
HEAD_DIM = 64
SC_KERNEL = 3
CF_KERNEL = 31
FFN_KERNEL = 3
MOBA_BLOCK = 256
MOBA_TOPK = 3
ROPE_THETA = 500000.0
ROPE_DIM = 16
MLA_Q_RANK = 384
MLA_KV_RANK = 128
MLA_NOPE_DIM = 64
MLA_ROPE_DIM = 32
D_FF = 2816
EPS = 1e-6
NEG = -1e30
LOG2E = 1.4426950408889634

LANES = 128
SUBLANES = 8
VMEM_LIMIT = 56 * 1024 * 1024

TOKEN_TILE = 512
ATTN_TILE = MOBA_BLOCK
CONV_ROWS = 64
CF_HALO = 32
FFN_CHUNK = 256

_C_SC = 0
_C_CF = _C_SC + 3 * GROUP_WIDTH
_C_MQ = _C_CF + 2 * GROUP_WIDTH
_C_MK = _C_MQ + N_HEADS * LANES
_C_MV = _C_MK + N_HEADS * LANES
_C_CQ = _C_MV + N_HEADS * LANES
_C_CKV = _C_CQ + MLA_Q_RANK
_C_KR = _C_CKV + MLA_KV_RANK
_C_END = _C_KR + LANES


def _const_spec(shape):
    zeros = (0,) * len(shape)
    return pl.BlockSpec(shape, lambda *_: zeros, pipeline_mode=pl.Buffered(1))


def _rms(x, g):
    return x * lax.rsqrt(jnp.mean(x * x, axis=-1, keepdims=True) + EPS) * g


def _sigmoid(x):
    return 1.0 / (1.0 + jnp.exp(-x))


def _rope(x, tab_ref, shift):
    return (x * tab_ref[0]
            + pltpu.roll(x, LANES - shift, 1) * tab_ref[1]
            + pltpu.roll(x, shift, 1) * tab_ref[2])


def _carry_halo(ext_ref, halo, rows, first):
    @pl.when(first)
    def _():
        ext_ref[0:halo, :] = jnp.zeros((halo, ext_ref.shape[1]), F32)

    @pl.when(jnp.logical_not(first))
    def _():
        ext_ref[0:halo, :] = ext_ref[rows:rows + halo, :]


def _mix_in_kernel(x_ref, g_ref, w_ref, wuq_ref, wukv_ref, gq_ref, gkv_ref,
                   ca_ref, cb_ref, cbb_ref, lng_ref, lnb_ref, tc_ref, td_ref, oh_ref,
                   ya_ref, yb_ref, qc_ref, kc_ref, vc_ref, qd_ref, kd_ref, vd_ref,
                   exta_ref, extb_ref):
    tm = x_ref.shape[1]
    first = pl.program_id(1) == 0
    h = _rms(x_ref[0], g_ref[...]).astype(BF16)

    def proj(a, b):
        return jnp.dot(h, w_ref[:, a:b], preferred_element_type=F32)

    hi_ones = (lax.broadcasted_iota(jnp.int32, (tm, LANES), 1) >= HEAD_DIM).astype(F32)

    z = proj(_C_SC, _C_CF)
    _carry_halo(exta_ref, SUBLANES, tm, first)
    exta_ref[SUBLANES:SUBLANES + tm, :] = z[:, 2 * GROUP_WIDTH:] * z[:, :GROUP_WIDTH]
    conv = ca_ref[0:1, :] * exta_ref[SUBLANES - 2:SUBLANES - 2 + tm, :]
    conv += ca_ref[1:2, :] * exta_ref[SUBLANES - 1:SUBLANES - 1 + tm, :]
    conv += ca_ref[2:3, :] * exta_ref[SUBLANES:SUBLANES + tm, :]
    ya_ref[0] = (z[:, GROUP_WIDTH:2 * GROUP_WIDTH] * conv).astype(BF16)

    z = proj(_C_CF, _C_MQ)
    _carry_halo(extb_ref, CF_HALO, tm, first)
    extb_ref[CF_HALO:CF_HALO + tm, :] = z[:, :GROUP_WIDTH] * _sigmoid(z[:, GROUP_WIDTH:])
    for r in range(tm // CONV_ROWS):
        base = r * CONV_ROWS + CF_HALO - (CF_KERNEL - 1)
        acc = jnp.broadcast_to(cbb_ref[...], (CONV_ROWS, GROUP_WIDTH))
        for k in range(CF_KERNEL):
            acc = acc + cb_ref[k:k + 1, :] * extb_ref[base + k:base + k + CONV_ROWS, :]
        xc = acc - jnp.mean(acc, axis=-1, keepdims=True)
        y = xc * lax.rsqrt(jnp.mean(xc * xc, axis=-1, keepdims=True) + EPS) * lng_ref[...] + lnb_ref[...]
        yb_ref[0, r * CONV_ROWS:(r + 1) * CONV_ROWS, :] = (y * _sigmoid(y)).astype(BF16)

    zq = proj(_C_MQ, _C_MK)
    zk = proj(_C_MK, _C_MV)
    zv = proj(_C_MV, _C_CQ)
    scale_c = HEAD_DIM ** -0.5 * LOG2E
    for hd in range(N_HEADS):
        sl = slice(hd * LANES, (hd + 1) * LANES)
        qc_ref[0, hd] = (_rope(zq[:, sl], tc_ref, ROPE_DIM // 2) * scale_c).astype(BF16)
        kc_ref[0, hd] = (_rope(zk[:, sl], tc_ref, ROPE_DIM // 2) + oh_ref[...]).astype(BF16)
        vc_ref[0, hd] = (zv[:, sl] + hi_ones).astype(BF16)

    z = proj(_C_CQ, _C_END)
    cq = _rms(z[:, :MLA_Q_RANK], gq_ref[...]).astype(BF16)
    ckv = _rms(z[:, MLA_Q_RANK:MLA_Q_RANK + MLA_KV_RANK], gkv_ref[...]).astype(BF16)
    kr = _rope(z[:, MLA_Q_RANK + MLA_KV_RANK:], td_ref, MLA_ROPE_DIM // 2)
    qd = jnp.dot(cq, wuq_ref[...], preferred_element_type=F32)
    kvd = jnp.dot(ckv, wukv_ref[...], preferred_element_type=F32)
    scale_d = (MLA_NOPE_DIM + MLA_ROPE_DIM) ** -0.5 * LOG2E
    for hd in range(N_HEADS):
        sl = slice(hd * LANES, (hd + 1) * LANES)
        slv = slice((N_HEADS + hd) * LANES, (N_HEADS + hd + 1) * LANES)
        qd_ref[0, hd] = (_rope(qd[:, sl], td_ref, MLA_ROPE_DIM // 2) * scale_d).astype(BF16)
        kd_ref[0, hd] = (kvd[:, sl] + kr).astype(BF16)
        vd_ref[0, hd] = (kvd[:, slv] + hi_ones).astype(BF16)


def _mix_in(x, g, w, wuq, wukv, gq, gkv, ca, cb, cbb, lng, lnb, tc, td, oh):
    b, s, _ = x.shape
    tm = TOKEN_TILE
    head_shape = jax.ShapeDtypeStruct((b, N_HEADS, s, LANES), BF16)
    head_spec = pl.BlockSpec((1, N_HEADS, tm, LANES), lambda bi, i: (bi, 0, i, 0))
    group_shape = jax.ShapeDtypeStruct((b, s, GROUP_WIDTH), BF16)
    group_spec = pl.BlockSpec((1, tm, GROUP_WIDTH), lambda bi, i: (bi, i, 0))
    tab_spec = pl.BlockSpec((3, tm, LANES), lambda bi, i: (0, i, 0))
    return pl.pallas_call(
        _mix_in_kernel,
        grid=(b, s // tm),
        in_specs=[
            pl.BlockSpec((1, tm, D_MODEL), lambda bi, i: (bi, i, 0)),
            _const_spec(g.shape), _const_spec(w.shape), _const_spec(wuq.shape), _const_spec(wukv.shape),
            _const_spec(gq.shape), _const_spec(gkv.shape), _const_spec(ca.shape), _const_spec(cb.shape),
            _const_spec(cbb.shape), _const_spec(lng.shape), _const_spec(lnb.shape),
            tab_spec, tab_spec, pl.BlockSpec((tm, LANES), lambda bi, i: (i, 0)),
        ],
        out_specs=[group_spec, group_spec] + [head_spec] * 6,
        out_shape=[group_shape, group_shape] + [head_shape] * 6,
        scratch_shapes=[pltpu.VMEM((tm + SUBLANES, GROUP_WIDTH), F32),
                        pltpu.VMEM((tm + CF_HALO, GROUP_WIDTH), F32)],
        compiler_params=pltpu.CompilerParams(
            dimension_semantics=("arbitrary", "arbitrary"), vmem_limit_bytes=VMEM_LIMIT),
        name="mix_in",
    )(x, g, w, wuq, wukv, gq, gkv, ca, cb, cbb, lng, lnb, tc, td, oh)


_NT = (((1,), (1,)), ((), ()))


def _block_means(k_ref, hd, gmat_ref):
    nblk = k_ref.shape[2] // MOBA_BLOCK
    keep = (lax.broadcasted_iota(jnp.int32, (1, LANES), 1) < HEAD_DIM).astype(F32)
    gmat_ref[hd] = jnp.zeros((LANES, LANES), F32)
    for n in range(nblk):
        blk = k_ref[0, hd, n * MOBA_BLOCK:(n + 1) * MOBA_BLOCK, :].astype(F32)
        gmat_ref[hd, HEAD_DIM + n:HEAD_DIM + n + 1, :] = jnp.mean(blk, axis=0, keepdims=True) * keep


def _select_bias(gate, own):
    lane = lax.broadcasted_iota(jnp.int32, gate.shape, 1)
    lane_f = lane.astype(F32)
    valid = jnp.logical_and(lane >= HEAD_DIM, lane < HEAD_DIM + own)
    g = jnp.where(valid, gate, -jnp.inf)
    keep = jnp.zeros(gate.shape, F32)
    for _ in range(MOBA_TOPK):
        top = jnp.max(g, axis=-1, keepdims=True)
        first = jnp.min(jnp.where(g == top, lane_f, float(2 * LANES)), axis=-1, keepdims=True)
        pick = lane_f == first
        keep = jnp.where(pick, 1.0, keep)
        g = jnp.where(pick, -jnp.inf, g)
    gate_lane = jnp.logical_and(lane >= HEAD_DIM, lane < HEAD_DIM + MOBA_BLOCK // SUBLANES)
    masked = jnp.logical_and(gate_lane, jnp.logical_not(jnp.logical_and(valid, keep > 0.5)))
    return jnp.where(masked, NEG, 0.0)


def _attn_kernel(q_ref, k_ref, v_ref, o_ref, *scratch, moba):
    nh, tq = q_ref.shape[1], q_ref.shape[2]
    qi = pl.program_id(1)
    row = lax.broadcasted_iota(jnp.int32, (tq, MOBA_BLOCK), 0)
    col = lax.broadcasted_iota(jnp.int32, (tq, MOBA_BLOCK), 1)
    causal = col <= row
    lane = lax.broadcasted_iota(jnp.int32, (tq, LANES), 1)
    own = pl.multiple_of(qi * MOBA_BLOCK, MOBA_BLOCK)

    m_ref, acc_ref = scratch[0], scratch[1]

    def probs(s, m):
        return jnp.concatenate([jnp.exp2(s[:, :LANES] - m), jnp.exp2(s[:, LANES:] - m)], axis=1).astype(BF16)

    q_past = []
    for hd in range(nh):
        q = q_ref[0, hd]
        if moba:
            gmat_ref = scratch[4]

            @pl.when(qi == 0)
            def _():
                _block_means(k_ref, hd, gmat_ref)

            gate = lax.dot_general(q, gmat_ref[hd].astype(BF16), _NT, preferred_element_type=F32)
            q_past.append((q.astype(F32) + _select_bias(gate, qi)).astype(BF16))
        else:
            q_past.append(q)
        s = lax.dot_general(q, k_ref[0, hd, pl.ds(own, MOBA_BLOCK), :], _NT, preferred_element_type=F32)
        s = jnp.where(causal, s, NEG)
        m = jnp.broadcast_to(jnp.max(s, axis=-1, keepdims=True), (tq, LANES))
        m_ref[hd] = m
        acc_ref[hd] = jnp.dot(probs(s, m), v_ref[0, hd, pl.ds(own, MOBA_BLOCK), :], preferred_element_type=F32)

    sa_ref, sb_ref = scratch[2], scratch[3]
    last_blk = k_ref.shape[2] // MOBA_BLOCK - 1

    def block_start(blk):
        return pl.multiple_of(jnp.minimum(blk, last_blk) * MOBA_BLOCK, MOBA_BLOCK)

    def scores_into(dst_ref, blk):
        start = block_start(blk)
        for hd in range(nh):
            dst_ref[hd] = lax.dot_general(q_past[hd], k_ref[0, hd, pl.ds(start, MOBA_BLOCK), :], _NT,
                                          preferred_element_type=F32)

    def consume(src_ref, blk, guarded):
        start = block_start(blk)
        off = jnp.where(blk < qi, 0.0, -NEG) if guarded else None
        for hd in range(nh):
            m = m_ref[hd]
            smax = jnp.max(src_ref[hd], axis=-1, keepdims=True)
            m_new = jnp.maximum(m, smax - off if guarded else smax)
            m_ref[hd] = m_new
            acc_ref[hd] = jnp.exp2(m - m_new) * acc_ref[hd] + jnp.dot(
                probs(src_ref[hd], m_new + off if guarded else m_new),
                v_ref[0, hd, pl.ds(start, MOBA_BLOCK), :], preferred_element_type=F32)

    scores_into(sa_ref, 0)

    def block_pair(t, carry):
        scores_into(sb_ref, 2 * t + 1)
        consume(sa_ref, 2 * t, False)
        scores_into(sa_ref, 2 * t + 2)
        consume(sb_ref, 2 * t + 1, True)
        return carry

    lax.fori_loop(0, lax.shift_right_logical(qi + 1, 1), block_pair, 0)
    for pair in range(nh // 2):
        acc0, acc1 = acc_ref[2 * pair], acc_ref[2 * pair + 1]
        sw0, sw1 = pltpu.roll(acc0, HEAD_DIM, 1), pltpu.roll(acc1, HEAD_DIM, 1)
        o_ref[0, :, pair * LANES:(pair + 1) * LANES] = jnp.where(
            lane < HEAD_DIM, acc0 / sw0, sw1 / acc1).astype(BF16)


def _attention(q, k, v, moba):
    b, nh, s, _ = q.shape
    tq = ATTN_TILE
    kv_spec = pl.BlockSpec((1, nh, s, LANES), lambda bi, i: (bi, 0, 0, 0), pipeline_mode=pl.Buffered(1))
    return pl.pallas_call(
        functools.partial(_attn_kernel, moba=moba),
        grid=(b, s // tq),
        in_specs=[pl.BlockSpec((1, nh, tq, LANES), lambda bi, i: (bi, 0, i, 0)), kv_spec, kv_spec],
        out_specs=pl.BlockSpec((1, tq, nh * HEAD_DIM), lambda bi, i: (bi, i, 0)),
        out_shape=jax.ShapeDtypeStruct((b, s, nh * HEAD_DIM), BF16),
        scratch_shapes=[pltpu.VMEM((nh, tq, LANES), F32), pltpu.VMEM((nh, tq, LANES), F32),
                        pltpu.VMEM((nh, tq, MOBA_BLOCK), F32), pltpu.VMEM((nh, tq, MOBA_BLOCK), F32)]
        + ([pltpu.VMEM((nh, LANES, LANES), F32)] if moba else []),
        compiler_params=pltpu.CompilerParams(
            dimension_semantics=("arbitrary", "arbitrary"), vmem_limit_bytes=VMEM_LIMIT),
        name="moba_attn" if moba else "mla_attn",
    )(q, k, v)


def _mix_out_kernel(x_ref, ya_ref, yb_ref, yc_ref, yd_ref, wo_ref, gpost_ref, gpre_ref, gfpost_ref,
                    wup_ref, cw_ref, wdn_ref, o_ref, ext_ref, carry_ref):
    tm = x_ref.shape[1]
    first = pl.program_id(1) == 0
    gw = GROUP_WIDTH
    y = jnp.dot(ya_ref[0], wo_ref[0:gw, :], preferred_element_type=F32)
    y += jnp.dot(yb_ref[0], wo_ref[gw:2 * gw, :], preferred_element_type=F32)
    y += jnp.dot(yc_ref[0], wo_ref[2 * gw:3 * gw, :], preferred_element_type=F32)
    y += jnp.dot(yd_ref[0], wo_ref[3 * gw:4 * gw, :], preferred_element_type=F32)
    x1 = x_ref[0] + _rms(y, gpost_ref[...])
    hn = _rms(x1, gpre_ref[...]).astype(BF16)

    @pl.when(first)
    def _():
        carry_ref[...] = jnp.zeros(carry_ref.shape, F32)

    fc = FFN_CHUNK
    f = jnp.zeros((tm, D_MODEL), F32)
    for c in range(D_FF // fc):
        convs = []
        for part in range(2):
            c0 = part * D_FF + c * fc
            ext = ext_ref.at[c % 2, part]
            ext[0:SUBLANES, :] = carry_ref[:, c0:c0 + fc]
            ext[SUBLANES:SUBLANES + tm, :] = jnp.dot(hn, wup_ref[:, c0:c0 + fc], preferred_element_type=F32)
            carry_ref[:, c0:c0 + fc] = ext[tm:tm + SUBLANES, :]
            conv = cw_ref[0:1, c0:c0 + fc] * ext[SUBLANES - 2:SUBLANES - 2 + tm, :]
            conv += cw_ref[1:2, c0:c0 + fc] * ext[SUBLANES - 1:SUBLANES - 1 + tm, :]
            conv += cw_ref[2:3, c0:c0 + fc] * ext[SUBLANES:SUBLANES + tm, :]
            convs.append(conv)
        act = (convs[0] * _sigmoid(convs[0]) * convs[1]).astype(BF16)
        f += jnp.dot(act, wdn_ref[c * fc:(c + 1) * fc, :], preferred_element_type=F32)
    o_ref[0] = x1 + _rms(f, gfpost_ref[...])


def _mix_out(x, ya, yb, yc, yd, wo, gpost, gpre, gfpost, wup, cw, wdn):
    b, s, _ = x.shape
    tm = TOKEN_TILE
    x_spec = pl.BlockSpec((1, tm, D_MODEL), lambda bi, i: (bi, i, 0))
    group_spec = pl.BlockSpec((1, tm, GROUP_WIDTH), lambda bi, i: (bi, i, 0))
    return pl.pallas_call(
        _mix_out_kernel,
        grid=(b, s // tm),
        in_specs=[x_spec, group_spec, group_spec, group_spec, group_spec,
                  _const_spec(wo.shape), _const_spec(gpost.shape), _const_spec(gpre.shape),
                  _const_spec(gfpost.shape), _const_spec(wup.shape), _const_spec(cw.shape),
                  _const_spec(wdn.shape)],
        out_specs=x_spec,
        out_shape=jax.ShapeDtypeStruct(x.shape, F32),
        scratch_shapes=[pltpu.VMEM((2, 2, tm + SUBLANES, FFN_CHUNK), F32),
                        pltpu.VMEM((SUBLANES, 2 * D_FF), F32)],
        compiler_params=pltpu.CompilerParams(
            dimension_semantics=("arbitrary", "arbitrary"), vmem_limit_bytes=VMEM_LIMIT),
        name="mix_out",
    )(x, ya, yb, yc, yd, wo, gpost, gpre, gfpost, wup, cw, wdn)


def _pad_heads(w, widths):
    per_head = sum(widths)
    w = w.reshape(w.shape[:-1] + (N_HEADS, per_head))
    w = jnp.pad(w, [(0, 0)] * (w.ndim - 1) + [(0, LANES - per_head)])
    return w.reshape(w.shape[:-2] + (N_HEADS * LANES,))


def _rope_tables(seq, dim, first_lane):
    half = dim // 2
    inv = 1.0 / (ROPE_THETA ** (jnp.arange(0, dim, 2, dtype=F32) / dim))
    ang = jnp.arange(seq, dtype=F32)[:, None] * inv[None, :]
    cos, sin = jnp.cos(ang), jnp.sin(ang)
    zero = jnp.zeros((seq, LANES), F32)
    c = jnp.ones((seq, LANES), F32)
    c = c.at[:, first_lane:first_lane + half].set(cos).at[:, first_lane + half:first_lane + dim].set(cos)
    s1 = zero.at[:, first_lane:first_lane + half].set(-sin)
    s2 = zero.at[:, first_lane + half:first_lane + dim].set(sin)
    return jnp.stack([c, s1, s2])


def _prep_w_in(w):
    sc, cf = w[:, :768], w[:, 768:1280]
    mq, mk, mv = w[:, 1280:1536], w[:, 1536:1792], w[:, 1792:2048]
    cq, ckv, kr = w[:, 2048:2432], w[:, 2432:2560], w[:, 2560:2592]
    kr = jnp.pad(kr, ((0, 0), (MLA_NOPE_DIM, LANES - MLA_NOPE_DIM - MLA_ROPE_DIM)))
    parts = [sc, cf, _pad_heads(mq, (HEAD_DIM,)), _pad_heads(mk, (HEAD_DIM,)), _pad_heads(mv, (HEAD_DIM,)),
             cq, ckv, kr]
    return jnp.concatenate(parts, axis=1).astype(BF16)


def _prep_w_ukv(w):
    w = w.reshape(MLA_KV_RANK, N_HEADS, 2, HEAD_DIM)
    k = _pad_heads(w[:, :, 0, :].reshape(MLA_KV_RANK, -1), (HEAD_DIM,))
    v = _pad_heads(w[:, :, 1, :].reshape(MLA_KV_RANK, -1), (HEAD_DIM,))
    return jnp.concatenate([k, v], axis=1).astype(BF16)


def kernel(x, norm_mix_pre, norm_mix_post, norm_ffn_pre, norm_ffn_post, w_in, conv_a, conv_b, conv_b_bias,
           ln_b_gain, ln_b_bias, mla_q_norm, mla_kv_norm, w_uq, w_ukv, w_out, w_ffn_up, conv_ffn, w_ffn_down):
    b, s, d = x.shape
    depth = w_in.shape[0]
    assert d == D_MODEL and s % TOKEN_TILE == 0 and s // MOBA_BLOCK <= MOBA_BLOCK // SUBLANES
    tc = _rope_tables(s, ROPE_DIM, 0)
    td = _rope_tables(s, MLA_ROPE_DIM, MLA_NOPE_DIM)
    blk = jnp.arange(s, dtype=jnp.int32)[:, None] // MOBA_BLOCK + HEAD_DIM
    oh = (blk == jnp.arange(LANES, dtype=jnp.int32)[None, :]).astype(F32)
    row = lambda v: v.reshape(1, -1)
    for l in range(depth):
        ya, yb, qc, kc, vc, qd, kd, vd = _mix_in(
            x, row(norm_mix_pre[l]), _prep_w_in(w_in[l]),
            _pad_heads(w_uq[l], (MLA_NOPE_DIM, MLA_ROPE_DIM)).astype(BF16), _prep_w_ukv(w_ukv[l]),
            row(mla_q_norm[l]), row(mla_kv_norm[l]), conv_a[l], conv_b[l], row(conv_b_bias[l]),
            row(ln_b_gain[l]), row(ln_b_bias[l]), tc, td, oh)
        yc = _attention(qc, kc, vc, True)
        yd = _attention(qd, kd, vd, False)
        x = _mix_out(x, ya, yb, yc, yd, w_out[l].astype(BF16), row(norm_mix_post[l]), row(norm_ffn_pre[l]),
                     row(norm_ffn_post[l]), w_ffn_up[l].astype(BF16), conv_ffn[l], w_ffn_down[l].astype(BF16))
    return x
```

```python
import functools

import jax
import jax.numpy as jnp
import numpy as np
from jax import lax
from jax.experimental import pallas as pl
from jax.experimental.pallas import tpu as pltpu

F32 = jnp.float32
BF16 = jnp.bfloat16

D_MODEL = 1024
GROUP_WIDTH = 256
N_HEADS = 4
HEAD_DIM = 64
SC_KERNEL = 3
CF_KERNEL = 31
FFN_KERNEL = 3
MOBA_BLOCK = 256
MOBA_TOPK = 3
ROPE_THETA = 500000.0
ROPE_DIM = 16
MLA_Q_RANK = 384
MLA_KV_RANK = 128
MLA_NOPE_DIM = 64
MLA_ROPE_DIM = 32
D_FF = 2816
EPS = 1e-6
NEG = -1e30
LOG2E = 1.4426950408889634

LANES = 128
SUBLANES = 8
VMEM_LIMIT = 56 * 1024 * 1024

TOKEN_TILE = 512
ATTN_TILE = MOBA_BLOCK
CONV_ROWS = 64
CF_HALO = 32
FFN_CHUNK = 256

_C_SC = 0
_C_CF = _C_SC + 3 * GROUP_WIDTH
_C_MQ = _C_CF + 2 * GROUP_WIDTH
_C_MK = _C_MQ + N_HEADS * LANES
_C_MV = _C_MK + N_HEADS * LANES
_C_CQ = _C_MV + GROUP_WIDTH
_C_CKV = _C_CQ + MLA_Q_RANK
_C_KR = _C_CKV + MLA_KV_RANK
_C_END = _C_KR + LANES


def _layer_spec(stacked, layer):
    zeros = (0,) * (stacked.ndim - 1)
    return pl.BlockSpec((None,) + stacked.shape[1:], lambda *_: (layer,) + zeros, pipeline_mode=pl.Buffered(1))


def _rms(x, g):
    return x * lax.rsqrt(jnp.mean(x * x, axis=-1, keepdims=True) + EPS) * g


def _sigmoid(x):
    return 1.0 / (1.0 + jnp.exp(-x))


def _rope(x, tab_ref, shift):
    return (x * tab_ref[0]
            + pltpu.roll(x, LANES - shift, 1) * tab_ref[1]
            + pltpu.roll(x, shift, 1) * tab_ref[2])


def _carry_halo(ext_ref, halo, rows, first):
    @pl.when(first)
    def _():
        ext_ref[0:halo, :] = jnp.zeros((halo, ext_ref.shape[1]), F32)

    @pl.when(jnp.logical_not(first))
    def _():
        ext_ref[0:halo, :] = ext_ref[rows:rows + halo, :]


def _mix_in_kernel(x_ref, g_ref, w_ref, wuq_ref, wukv_ref, gq_ref, gkv_ref,
                   ca_ref, cb_ref, cbb_ref, lng_ref, lnb_ref, tc_ref, td_ref, oh_ref,
                   ya_ref, yb_ref, qc_ref, kc_ref, vc_ref, qd_ref, kd_ref, vd_ref,
                   exta_ref, extb_ref):
    tm = x_ref.shape[1]
    first = pl.program_id(1) == 0
    h = _rms(x_ref[0], g_ref[...]).astype(BF16)

    def proj(a, b):
        return jnp.dot(h, w_ref[:, a:b], preferred_element_type=F32)

    z = proj(_C_SC, _C_CF)
    _carry_halo(exta_ref, SUBLANES, tm, first)
    exta_ref[SUBLANES:SUBLANES + tm, :] = z[:, 2 * GROUP_WIDTH:] * z[:, :GROUP_WIDTH]
    conv = ca_ref[0:1, :] * exta_ref[SUBLANES - 2:SUBLANES - 2 + tm, :]
    conv += ca_ref[1:2, :] * exta_ref[SUBLANES - 1:SUBLANES - 1 + tm, :]
    conv += ca_ref[2:3, :] * exta_ref[SUBLANES:SUBLANES + tm, :]
    ya_ref[0] = (z[:, GROUP_WIDTH:2 * GROUP_WIDTH] * conv).astype(BF16)

    z = proj(_C_CF, _C_MQ)
    _carry_halo(extb_ref, CF_HALO, tm, first)
    extb_ref[CF_HALO:CF_HALO + tm, :] = z[:, :GROUP_WIDTH] * _sigmoid(z[:, GROUP_WIDTH:])
    for r in range(tm // CONV_ROWS):
        base = r * CONV_ROWS + CF_HALO - (CF_KERNEL - 1)
        acc = jnp.broadcast_to(cbb_ref[...], (CONV_ROWS, GROUP_WIDTH))
        for k in range(CF_KERNEL):
            acc = acc + cb_ref[k:k + 1, :] * extb_ref[base + k:base + k + CONV_ROWS, :]
        xc = acc - jnp.mean(acc, axis=-1, keepdims=True)
        y = xc * lax.rsqrt(jnp.mean(xc * xc, axis=-1, keepdims=True) + EPS) * lng_ref[...] + lnb_ref[...]
        yb_ref[0, r * CONV_ROWS:(r + 1) * CONV_ROWS, :] = (y * _sigmoid(y)).astype(BF16)

    zq = proj(_C_MQ, _C_MK)
    zk = proj(_C_MK, _C_MV)
    zv = proj(_C_MV, _C_CQ)
    scale_c = HEAD_DIM ** -0.5 * LOG2E
    for hd in range(N_HEADS):
        sl = slice(hd * LANES, (hd + 1) * LANES)
        qc_ref[0, hd] = (_rope(zq[:, sl], tc_ref, ROPE_DIM // 2) * scale_c).astype(BF16)
        kc_ref[0, hd] = (_rope(zk[:, sl], tc_ref, ROPE_DIM // 2) + oh_ref[...]).astype(BF16)
    for blk in range(tm // MOBA_BLOCK):
        vc_ref[0, blk] = zv[blk * MOBA_BLOCK:(blk + 1) * MOBA_BLOCK, :].T.astype(BF16)

    z = proj(_C_CQ, _C_END)
    cq = _rms(z[:, :MLA_Q_RANK], gq_ref[...]).astype(BF16)
    ckv = _rms(z[:, MLA_Q_RANK:MLA_Q_RANK + MLA_KV_RANK], gkv_ref[...]).astype(BF16)
    kr = _rope(z[:, MLA_Q_RANK + MLA_KV_RANK:], td_ref, MLA_ROPE_DIM // 2)
    qd = jnp.dot(cq, wuq_ref[...], preferred_element_type=F32)
    kvd = jnp.dot(ckv, wukv_ref[...], preferred_element_type=F32)
    scale_d = (MLA_NOPE_DIM + MLA_ROPE_DIM) ** -0.5 * LOG2E
    for hd in range(N_HEADS):
        sl = slice(hd * LANES, (hd + 1) * LANES)
        qd_ref[0, hd] = (_rope(qd[:, sl], td_ref, MLA_ROPE_DIM // 2) * scale_d).astype(BF16)
        kd_ref[0, hd] = (kvd[:, sl] + kr).astype(BF16)
    vd = kvd[:, N_HEADS * LANES:]
    for blk in range(tm // MOBA_BLOCK):
        vd_ref[0, blk] = vd[blk * MOBA_BLOCK:(blk + 1) * MOBA_BLOCK, :].T.astype(BF16)


def _mix_in(layer, x, g, w, wuq, wukv, gq, gkv, ca, cb, cbb, lng, lnb, tc, td, oh):
    b, s, _ = x.shape
    params = (g, w, wuq, wukv, gq, gkv, ca, cb, cbb, lng, lnb)
    tm = TOKEN_TILE
    head_shape = jax.ShapeDtypeStruct((b, N_HEADS, s, LANES), BF16)
    head_spec = pl.BlockSpec((1, N_HEADS, tm, LANES), lambda bi, i: (bi, 0, i, 0))
    group_shape = jax.ShapeDtypeStruct((b, s, GROUP_WIDTH), BF16)
    group_spec = pl.BlockSpec((1, tm, GROUP_WIDTH), lambda bi, i: (bi, i, 0))
    tab_spec = pl.BlockSpec((3, tm, LANES), lambda bi, i: (0, i, 0))
    vt_shape = jax.ShapeDtypeStruct((b, s // MOBA_BLOCK, GROUP_WIDTH, MOBA_BLOCK), BF16)
    vt_spec = pl.BlockSpec((1, tm // MOBA_BLOCK, GROUP_WIDTH, MOBA_BLOCK), lambda bi, i: (bi, i, 0, 0))
    return pl.pallas_call(
        _mix_in_kernel,
        grid=(b, s // tm),
        in_specs=[pl.BlockSpec((1, tm, D_MODEL), lambda bi, i: (bi, i, 0))]
        + [_layer_spec(p, layer) for p in params]
        + [tab_spec, tab_spec, pl.BlockSpec((tm, LANES), lambda bi, i: (i, 0))],
        out_specs=[group_spec, group_spec] + [head_spec, head_spec, vt_spec] * 2,
        out_shape=[group_shape, group_shape] + [head_shape, head_shape, vt_shape] * 2,
        scratch_shapes=[pltpu.VMEM((tm + SUBLANES, GROUP_WIDTH), F32),
                        pltpu.VMEM((tm + CF_HALO, GROUP_WIDTH), F32)],
        compiler_params=pltpu.CompilerParams(
            dimension_semantics=("arbitrary", "arbitrary"), vmem_limit_bytes=VMEM_LIMIT),
        name="mix_in",
    )(x, g, w, wuq, wukv, gq, gkv, ca, cb, cbb, lng, lnb, tc, td, oh)


_NT = (((1,), (1,)), ((), ()))


def _block_means(k_ref, hd, gmat_ref):
    nblk = k_ref.shape[2] // MOBA_BLOCK
    keep = (lax.broadcasted_iota(jnp.int32, (1, LANES), 1) < HEAD_DIM).astype(F32)
    gmat_ref[hd] = jnp.zeros((LANES, LANES), F32)
    for n in range(nblk):
        blk = k_ref[0, hd, n * MOBA_BLOCK:(n + 1) * MOBA_BLOCK, :].astype(F32)
        gmat_ref[hd, HEAD_DIM + n:HEAD_DIM + n + 1, :] = jnp.mean(blk, axis=0, keepdims=True) * keep


def _select_bias(gate, own):
    lane = lax.broadcasted_iota(jnp.int32, gate.shape, 1)
    lane_f = lane.astype(F32)
    valid = jnp.logical_and(lane >= HEAD_DIM, lane < HEAD_DIM + own)
    g = jnp.where(valid, gate, -jnp.inf)
    keep = jnp.zeros(gate.shape, F32)
    for _ in range(MOBA_TOPK):
        top = jnp.max(g, axis=-1, keepdims=True)
        first = jnp.min(jnp.where(g == top, lane_f, float(2 * LANES)), axis=-1, keepdims=True)
        pick = lane_f == first
        keep = jnp.where(pick, 1.0, keep)
        g = jnp.where(pick, -jnp.inf, g)
    gate_lane = jnp.logical_and(lane >= HEAD_DIM, lane < HEAD_DIM + MOBA_BLOCK // SUBLANES)
    masked = jnp.logical_and(gate_lane, jnp.logical_not(jnp.logical_and(valid, keep > 0.5)))
    return jnp.where(masked, NEG, 0.0)


def _attn_kernel(q_ref, k_ref, v_ref, o_ref, m_ref, l_ref, acc_ref, sa_ref, sb_ref, *gmat, moba):
    nh, tq = q_ref.shape[1], q_ref.shape[2]
    qi = pl.program_id(1)
    key = lax.broadcasted_iota(jnp.int32, (MOBA_BLOCK, tq), 0)
    qry = lax.broadcasted_iota(jnp.int32, (MOBA_BLOCK, tq), 1)
    causal = key <= qry
    own = pl.multiple_of(qi * MOBA_BLOCK, MOBA_BLOCK)
    last_blk = k_ref.shape[2] // MOBA_BLOCK - 1

    def values_t(blk, hd):
        return v_ref[0, blk, hd * HEAD_DIM:(hd + 1) * HEAD_DIM, :]

    if moba:
        gmat_ref = gmat[0]

        @pl.when(qi == 0)
        def _():
            for hd in range(nh):
                _block_means(k_ref, hd, gmat_ref)

    q_past = []
    for hd in range(nh):
        q = q_ref[0, hd]
        if moba:
            gate = lax.dot_general(q, gmat_ref[hd].astype(BF16), _NT, preferred_element_type=F32)
            q_past.append((q.astype(F32) + _select_bias(gate, qi)).astype(BF16))
        else:
            q_past.append(q)
        s = lax.dot_general(k_ref[0, hd, pl.ds(own, MOBA_BLOCK), :], q, _NT, preferred_element_type=F32)
        s = jnp.where(causal, s, NEG)
        m = jnp.max(s, axis=0, keepdims=True)
        p = jnp.exp2(s - m)
        m_ref[hd] = m
        l_ref[hd] = jnp.sum(p, axis=0, keepdims=True)
        acc_ref[hd] = jnp.dot(values_t(qi, hd), p.astype(BF16), preferred_element_type=F32)

    def scores_into(dst_ref, blk):
        start = pl.multiple_of(jnp.minimum(blk, last_blk) * MOBA_BLOCK, MOBA_BLOCK)
        for hd in range(nh):
            dst_ref[hd] = lax.dot_general(k_ref[0, hd, pl.ds(start, MOBA_BLOCK), :], q_past[hd], _NT,
                                          preferred_element_type=F32)

    def consume(src_ref, blk, guarded):
        off = jnp.where(blk < qi, 0.0, -NEG) if guarded else None
        vblk = jnp.minimum(blk, last_blk)
        for hd in range(nh):
            m = m_ref[hd]
            smax = jnp.max(src_ref[hd], axis=0, keepdims=True)
            m_new = jnp.maximum(m, smax - off if guarded else smax)
            alpha = jnp.exp2(m - m_new)
            p = jnp.exp2(src_ref[hd] - (m_new + off if guarded else m_new))
            m_ref[hd] = m_new
            l_ref[hd] = alpha * l_ref[hd] + jnp.sum(p, axis=0, keepdims=True)
            acc_ref[hd] = alpha * acc_ref[hd] + jnp.dot(values_t(vblk, hd), p.astype(BF16),
                                                        preferred_element_type=F32)

    scores_into(sa_ref, 0)

    def block_pair(t, carry):
        scores_into(sb_ref, 2 * t + 1)
        consume(sa_ref, 2 * t, False)
        scores_into(sa_ref, 2 * t + 2)
        consume(sb_ref, 2 * t + 1, True)
        return carry

    lax.fori_loop(0, lax.shift_right_logical(qi + 1, 1), block_pair, 0)
    out_t = jnp.concatenate([acc_ref[hd] * (1.0 / l_ref[hd]) for hd in range(nh)], axis=0)
    o_ref[0] = out_t.T.astype(BF16)


def _attention(q, k, vt, moba):
    b, nh, s, _ = q.shape
    tq = ATTN_TILE
    k_spec = pl.BlockSpec((1, nh, s, LANES), lambda bi, i: (bi, 0, 0, 0), pipeline_mode=pl.Buffered(1))
    v_spec = pl.BlockSpec((1,) + vt.shape[1:], lambda bi, i: (bi, 0, 0, 0), pipeline_mode=pl.Buffered(1))
    return pl.pallas_call(
        functools.partial(_attn_kernel, moba=moba),
        grid=(b, s // tq),
        in_specs=[pl.BlockSpec((1, nh, tq, LANES), lambda bi, i: (bi, 0, i, 0)), k_spec, v_spec],
        out_specs=pl.BlockSpec((1, tq, nh * HEAD_DIM), lambda bi, i: (bi, i, 0)),
        out_shape=jax.ShapeDtypeStruct((b, s, nh * HEAD_DIM), BF16),
        scratch_shapes=[pltpu.VMEM((nh, 1, tq), F32), pltpu.VMEM((nh, 1, tq), F32),
                        pltpu.VMEM((nh, HEAD_DIM, tq), F32),
                        pltpu.VMEM((nh, MOBA_BLOCK, tq), F32), pltpu.VMEM((nh, MOBA_BLOCK, tq), F32)]
        + ([pltpu.VMEM((nh, LANES, LANES), F32)] if moba else []),
        compiler_params=pltpu.CompilerParams(
            dimension_semantics=("arbitrary", "arbitrary"), vmem_limit_bytes=VMEM_LIMIT),
        name="moba_attn" if moba else "mla_attn",
    )(q, k, vt)


def _mix_out_kernel(x_ref, ya_ref, yb_ref, yc_ref, yd_ref, wo_ref, gpost_ref, gpre_ref, gfpost_ref,
                    wup_ref, cw_ref, wdn_ref, o_ref, ext_ref, carry_ref):
    tm = x_ref.shape[1]
    first = pl.program_id(1) == 0
    gw = GROUP_WIDTH
    y = jnp.dot(ya_ref[0], wo_ref[0:gw, :], preferred_element_type=F32)
    y += jnp.dot(yb_ref[0], wo_ref[gw:2 * gw, :], preferred_element_type=F32)
    y += jnp.dot(yc_ref[0], wo_ref[2 * gw:3 * gw, :], preferred_element_type=F32)
    y += jnp.dot(yd_ref[0], wo_ref[3 * gw:4 * gw, :], preferred_element_type=F32)
    x1 = x_ref[0] + _rms(y, gpost_ref[...])
    hn = _rms(x1, gpre_ref[...]).astype(BF16)

    @pl.when(first)
    def _():
        carry_ref[...] = jnp.zeros(carry_ref.shape, F32)

    fc = FFN_CHUNK
    f = jnp.zeros((tm, D_MODEL), F32)
    for c in range(D_FF // fc):
        convs = []
        for part in range(2):
            c0 = part * D_FF + c * fc
            ext = ext_ref.at[c % 2, part]
            ext[0:SUBLANES, :] = carry_ref[:, c0:c0 + fc]
            ext[SUBLANES:SUBLANES + tm, :] = jnp.dot(hn, wup_ref[:, c0:c0 + fc], preferred_element_type=F32)
            carry_ref[:, c0:c0 + fc] = ext[tm:tm + SUBLANES, :]
            conv = cw_ref[0:1, c0:c0 + fc] * ext[SUBLANES - 2:SUBLANES - 2 + tm, :]
            conv += cw_ref[1:2, c0:c0 + fc] * ext[SUBLANES - 1:SUBLANES - 1 + tm, :]
            conv += cw_ref[2:3, c0:c0 + fc] * ext[SUBLANES:SUBLANES + tm, :]
            convs.append(conv)
        act = (convs[0] * _sigmoid(convs[0]) * convs[1]).astype(BF16)
        f += jnp.dot(act, wdn_ref[c * fc:(c + 1) * fc, :], preferred_element_type=F32)
    o_ref[0] = x1 + _rms(f, gfpost_ref[...])


def _mix_out(layer, x, ya, yb, yc, yd, wo, gpost, gpre, gfpost, wup, cw, wdn):
    b, s, _ = x.shape
    params = (wo, gpost, gpre, gfpost, wup, cw, wdn)
    tm = TOKEN_TILE
    x_spec = pl.BlockSpec((1, tm, D_MODEL), lambda bi, i: (bi, i, 0))
    group_spec = pl.BlockSpec((1, tm, GROUP_WIDTH), lambda bi, i: (bi, i, 0))
    return pl.pallas_call(
        _mix_out_kernel,
        grid=(b, s // tm),
        in_specs=[x_spec, group_spec, group_spec, group_spec, group_spec]
        + [_layer_spec(p, layer) for p in params],
        out_specs=x_spec,
        out_shape=jax.ShapeDtypeStruct(x.shape, F32),
        scratch_shapes=[pltpu.VMEM((2, 2, tm + SUBLANES, FFN_CHUNK), F32),
                        pltpu.VMEM((SUBLANES, 2 * D_FF), F32)],
        compiler_params=pltpu.CompilerParams(
            dimension_semantics=("arbitrary", "arbitrary"), vmem_limit_bytes=VMEM_LIMIT),
        name="mix_out",
    )(x, ya, yb, yc, yd, wo, gpost, gpre, gfpost, wup, cw, wdn)


def _pad_heads(w, widths):
    per_head = sum(widths)
    w = w.reshape(w.shape[:-1] + (N_HEADS, per_head))
    w = jnp.pad(w, [(0, 0)] * (w.ndim - 1) + [(0, LANES - per_head)])
    return w.reshape(w.shape[:-2] + (N_HEADS * LANES,))


def _rope_tables(seq, dim, first_lane):
    half = dim // 2
    rel = np.arange(LANES) - first_lane
    rotary = (rel >= 0) & (rel < dim)
    inv = 1.0 / (ROPE_THETA ** (jnp.arange(0, dim, 2, dtype=F32) / dim))
    inv_lane = jnp.where(rotary, inv[np.where(rotary, rel % half, 0)], 0.0)
    ang = jnp.arange(seq, dtype=F32)[:, None] * inv_lane[None, :]
    sin = jnp.sin(ang)
    return jnp.stack([jnp.cos(ang),
                      jnp.where(rotary & (rel < half), -sin, 0.0),
                      jnp.where(rotary & (rel >= half), sin, 0.0)])


def _prep_w_in(w):
    sc, cf = w[..., :768], w[..., 768:1280]
    mq, mk, mv = w[..., 1280:1536], w[..., 1536:1792], w[..., 1792:2048]
    cq, ckv, kr = w[..., 2048:2432], w[..., 2432:2560], w[..., 2560:2592]
    kr = jnp.pad(kr, ((0, 0), (0, 0), (MLA_NOPE_DIM, LANES - MLA_NOPE_DIM - MLA_ROPE_DIM)))
    parts = [sc, cf, _pad_heads(mq, (HEAD_DIM,)), _pad_heads(mk, (HEAD_DIM,)), mv, cq, ckv, kr]
    return jnp.concatenate(parts, axis=-1).astype(BF16)


def _prep_w_ukv(w):
    w = w.reshape(w.shape[:-1] + (N_HEADS, 2, HEAD_DIM))
    k = _pad_heads(w[..., 0, :].reshape(w.shape[:2] + (-1,)), (HEAD_DIM,))
    v = w[..., 1, :].reshape(w.shape[:2] + (-1,))
    return jnp.concatenate([k, v], axis=-1).astype(BF16)


def kernel(x, norm_mix_pre, norm_mix_post, norm_ffn_pre, norm_ffn_post, w_in, conv_a, conv_b, conv_b_bias,
           ln_b_gain, ln_b_bias, mla_q_norm, mla_kv_norm, w_uq, w_ukv, w_out, w_ffn_up, conv_ffn, w_ffn_down):
    b, s, d = x.shape
    depth = w_in.shape[0]
    assert d == D_MODEL and s % TOKEN_TILE == 0 and s // MOBA_BLOCK <= MOBA_BLOCK // SUBLANES
    tc = _rope_tables(s, ROPE_DIM, 0)
    td = _rope_tables(s, MLA_ROPE_DIM, MLA_NOPE_DIM)
    blk = jnp.arange(s, dtype=jnp.int32)[:, None] // MOBA_BLOCK + HEAD_DIM
    oh = (blk == jnp.arange(LANES, dtype=jnp.int32)[None, :]).astype(F32)
    rows = lambda v: v.reshape(depth, 1, -1)
    in_params = (rows(norm_mix_pre), _prep_w_in(w_in), _pad_heads(w_uq, (MLA_NOPE_DIM, MLA_ROPE_DIM)).astype(BF16),
                 _prep_w_ukv(w_ukv), rows(mla_q_norm), rows(mla_kv_norm), conv_a, conv_b, rows(conv_b_bias),
                 rows(ln_b_gain), rows(ln_b_bias))
    out_params = (w_out.astype(BF16), rows(norm_mix_post), rows(norm_ffn_pre), rows(norm_ffn_post),
                  w_ffn_up.astype(BF16), conv_ffn, w_ffn_down.astype(BF16))
    for layer in range(depth):
        ya, yb, qc, kc, vc, qd, kd, vd = _mix_in(layer, x, *in_params, tc, td, oh)
        yc = _attention(qc, kc, vc, True)
        yd = _attention(qd, kd, vd, False)
        x = _mix_out(layer, x, ya, yb, yc, yd, *out_params)
    return x
```

```python
import functools

import jax
import jax.numpy as jnp
import numpy as np
from jax import lax
from jax.experimental import pallas as pl
from jax.experimental.pallas import tpu as pltpu

F32 = jnp.float32
BF16 = jnp.bfloat16

D_MODEL = 1024
GROUP_WIDTH = 256
N_HEADS = 4
HEAD_DIM = 64
SC_KERNEL = 3
CF_KERNEL = 31
FFN_KERNEL = 3
MOBA_BLOCK = 256
MOBA_TOPK = 3
ROPE_THETA = 500000.0
ROPE_DIM = 16
MLA_Q_RANK = 384
MLA_KV_RANK = 128
MLA_NOPE_DIM = 64
MLA_ROPE_DIM = 32
D_FF = 2816
EPS = 1e-6
NEG = -1e30
LOG2E = 1.4426950408889634

LANES = 128
SUBLANES = 8
VMEM_LIMIT = 56 * 1024 * 1024

TOKEN_TILE = 512
ATTN_TILE = MOBA_BLOCK
CONV_ROWS = 64
VT_ROWS = 80
CF_HALO = 32
FFN_CHUNK = 256
DOWN_CHUNKS = 6

_C_SC = 0
_C_CF = _C_SC + 3 * GROUP_WIDTH
_C_MQ = _C_CF + 2 * GROUP_WIDTH
_C_MK = _C_MQ + N_HEADS * LANES
_C_MV = _C_MK + N_HEADS * LANES
_C_CQ = _C_MV + GROUP_WIDTH
_C_CKV = _C_CQ + MLA_Q_RANK
_C_KR = _C_CKV + MLA_KV_RANK
_C_END = _C_KR + LANES


def _layer_spec(stacked, layer):
    zeros = (0,) * (stacked.ndim - 1)
    return pl.BlockSpec((None,) + stacked.shape[1:], lambda *_: (layer,) + zeros, pipeline_mode=pl.Buffered(1))


def _rms(x, g):
    return x * lax.rsqrt(jnp.mean(x * x, axis=-1, keepdims=True) + EPS) * g


def _sigmoid(x):
    return 1.0 / (1.0 + jnp.exp(-x))


def _rope(x, tab_ref, shift):
    return (x * tab_ref[0]
            + pltpu.roll(x, LANES - shift, 1) * tab_ref[1]
            + pltpu.roll(x, shift, 1) * tab_ref[2])


def _carry_halo(ext_ref, halo, rows, first):
    @pl.when(first)
    def _():
        ext_ref[0:halo, :] = jnp.zeros((halo, ext_ref.shape[1]), F32)

    @pl.when(jnp.logical_not(first))
    def _():
        ext_ref[0:halo, :] = ext_ref[rows:rows + halo, :]


def _store_values_t(vt_ref, v):
    ones = jnp.ones((VT_ROWS - HEAD_DIM, MOBA_BLOCK), BF16)
    for blk in range(v.shape[0] // MOBA_BLOCK):
        vt = v[blk * MOBA_BLOCK:(blk + 1) * MOBA_BLOCK, :].T.astype(BF16)
        for hd in range(N_HEADS):
            vt_ref[0, blk, hd * VT_ROWS:hd * VT_ROWS + HEAD_DIM, :] = vt[hd * HEAD_DIM:(hd + 1) * HEAD_DIM, :]
            vt_ref[0, blk, hd * VT_ROWS + HEAD_DIM:(hd + 1) * VT_ROWS, :] = ones


def _mix_in_kernel(x_ref, g_ref, w_ref, wuq_ref, wukv_ref, gq_ref, gkv_ref,
                   ca_ref, cb_ref, cbb_ref, lng_ref, lnb_ref, tc_ref, td_ref, oh_ref,
                   ya_ref, yb_ref, qc_ref, kc_ref, vc_ref, qd_ref, kd_ref, vd_ref,
                   exta_ref, extb_ref, zatt_ref, shift_ref):
    tm = x_ref.shape[1]
    first = pl.program_id(1) == 0
    h = _rms(x_ref[0], g_ref[...]).astype(BF16)

    def proj(a, b):
        return jnp.dot(h, w_ref[:, a:b], preferred_element_type=F32)

    z = proj(_C_SC, _C_CF)
    _carry_halo(exta_ref, SUBLANES, tm, first)
    exta_ref[SUBLANES:SUBLANES + tm, :] = z[:, 2 * GROUP_WIDTH:] * z[:, :GROUP_WIDTH]
    conv = ca_ref[0:1, :] * exta_ref[SUBLANES - 2:SUBLANES - 2 + tm, :]
    conv += ca_ref[1:2, :] * exta_ref[SUBLANES - 1:SUBLANES - 1 + tm, :]
    conv += ca_ref[2:3, :] * exta_ref[SUBLANES:SUBLANES + tm, :]
    ya_ref[0] = (z[:, GROUP_WIDTH:2 * GROUP_WIDTH] * conv).astype(BF16)

    z = proj(_C_CF, _C_MQ)
    _carry_halo(extb_ref, CF_HALO, tm, first)
    extb_ref[CF_HALO:CF_HALO + tm, :] = z[:, :GROUP_WIDTH] * _sigmoid(z[:, GROUP_WIDTH:])
    zatt_ref[...] = proj(_C_MQ, _C_END)
    zq = zatt_ref.at[:, 0:_C_MK - _C_MQ]
    zk = zatt_ref.at[:, _C_MK - _C_MQ:_C_MV - _C_MQ]
    zv = zatt_ref.at[:, _C_MV - _C_MQ:_C_CQ - _C_MQ]
    zd = zatt_ref.at[:, _C_CQ - _C_MQ:_C_END - _C_MQ]
    span = tm + CF_HALO - SUBLANES
    for j in range(1, SUBLANES):
        shift_ref[j - 1, 0:span, :] = extb_ref[j:j + span, :]
    for r in range(tm // CONV_ROWS):
        base = r * CONV_ROWS + CF_HALO - (CF_KERNEL - 1)
        acc = jnp.broadcast_to(cbb_ref[...], (CONV_ROWS, GROUP_WIDTH))
        for k in range(CF_KERNEL):
            offset = (base + k) % SUBLANES
            rows = extb_ref if offset == 0 else shift_ref.at[offset - 1]
            acc = acc + cb_ref[k:k + 1, :] * rows[base + k - offset:base + k - offset + CONV_ROWS, :]
        xc = acc - jnp.mean(acc, axis=-1, keepdims=True)
        y = xc * lax.rsqrt(jnp.mean(xc * xc, axis=-1, keepdims=True) + EPS) * lng_ref[...] + lnb_ref[...]
        yb_ref[0, r * CONV_ROWS:(r + 1) * CONV_ROWS, :] = (y * _sigmoid(y)).astype(BF16)

    scale_c = HEAD_DIM ** -0.5 * LOG2E
    for hd in range(N_HEADS):
        sl = slice(hd * LANES, (hd + 1) * LANES)
        qc_ref[0, hd] = (_rope(zq[:, sl], tc_ref, ROPE_DIM // 2) * scale_c).astype(BF16)
        kc_ref[0, hd] = (_rope(zk[:, sl], tc_ref, ROPE_DIM // 2) + oh_ref[...]).astype(BF16)
    _store_values_t(vc_ref, zv[...])

    cq = _rms(zd[:, :MLA_Q_RANK], gq_ref[...]).astype(BF16)
    ckv = _rms(zd[:, MLA_Q_RANK:MLA_Q_RANK + MLA_KV_RANK], gkv_ref[...]).astype(BF16)
    kr = _rope(zd[:, MLA_Q_RANK + MLA_KV_RANK:], td_ref, MLA_ROPE_DIM // 2)
    qd = jnp.dot(cq, wuq_ref[...], preferred_element_type=F32)
    kvd = jnp.dot(ckv, wukv_ref[...], preferred_element_type=F32)
    scale_d = (MLA_NOPE_DIM + MLA_ROPE_DIM) ** -0.5 * LOG2E
    for hd in range(N_HEADS):
        sl = slice(hd * LANES, (hd + 1) * LANES)
        qd_ref[0, hd] = (_rope(qd[:, sl], td_ref, MLA_ROPE_DIM // 2) * scale_d).astype(BF16)
        kd_ref[0, hd] = (kvd[:, sl] + kr).astype(BF16)
    _store_values_t(vd_ref, kvd[:, N_HEADS * LANES:])


def _mix_in(layer, x, g, w, wuq, wukv, gq, gkv, ca, cb, cbb, lng, lnb, tc, td, oh):
    b, s, _ = x.shape
    params = (g, w, wuq, wukv, gq, gkv, ca, cb, cbb, lng, lnb)
    tm = TOKEN_TILE
    head_shape = jax.ShapeDtypeStruct((b, N_HEADS, s, LANES), BF16)
    head_spec = pl.BlockSpec((1, N_HEADS, tm, LANES), lambda bi, i: (bi, 0, i, 0))
    group_shape = jax.ShapeDtypeStruct((b, s, GROUP_WIDTH), BF16)
    group_spec = pl.BlockSpec((1, tm, GROUP_WIDTH), lambda bi, i: (bi, i, 0))
    tab_spec = pl.BlockSpec((3, tm, LANES), lambda bi, i: (0, i, 0))
    vt_shape = jax.ShapeDtypeStruct((b, s // MOBA_BLOCK, N_HEADS * VT_ROWS, MOBA_BLOCK), BF16)
    vt_spec = pl.BlockSpec((1, tm // MOBA_BLOCK, N_HEADS * VT_ROWS, MOBA_BLOCK), lambda bi, i: (bi, i, 0, 0))
    return pl.pallas_call(
        _mix_in_kernel,
        grid=(b, s // tm),
        in_specs=[pl.BlockSpec((1, tm, D_MODEL), lambda bi, i: (bi, i, 0))]
        + [_layer_spec(p, layer) for p in params]
        + [tab_spec, tab_spec, pl.BlockSpec((tm, LANES), lambda bi, i: (i, 0))],
        out_specs=[group_spec, group_spec] + [head_spec, head_spec, vt_spec] * 2,
        out_shape=[group_shape, group_shape] + [head_shape, head_shape, vt_shape] * 2,
        scratch_shapes=[pltpu.VMEM((tm + SUBLANES, GROUP_WIDTH), F32),
                        pltpu.VMEM((tm + CF_HALO, GROUP_WIDTH), F32),
                        pltpu.VMEM((tm, _C_END - _C_MQ), F32),
                        pltpu.VMEM((SUBLANES - 1, tm + CF_HALO, GROUP_WIDTH), F32)],
        compiler_params=pltpu.CompilerParams(
            dimension_semantics=("arbitrary", "arbitrary"), vmem_limit_bytes=VMEM_LIMIT),
        name="mix_in",
    )(x, g, w, wuq, wukv, gq, gkv, ca, cb, cbb, lng, lnb, tc, td, oh)


_NT = (((1,), (1,)), ((), ()))


def _block_means(k_ref, hd, gmat_ref):
    nblk = k_ref.shape[2] // MOBA_BLOCK
    keep = (lax.broadcasted_iota(jnp.int32, (1, LANES), 1) < HEAD_DIM).astype(F32)
    gmat_ref[hd] = jnp.zeros((LANES, LANES), F32)
    for n in range(nblk):
        blk = k_ref[0, hd, n * MOBA_BLOCK:(n + 1) * MOBA_BLOCK, :].astype(F32)
        gmat_ref[hd, HEAD_DIM + n:HEAD_DIM + n + 1, :] = jnp.mean(blk, axis=0, keepdims=True) * keep


def _select_bias_t(gate_t, own):
    n_gate = MOBA_BLOCK // SUBLANES
    g = gate_t[HEAD_DIM:HEAD_DIM + n_gate, :]
    blk = lax.broadcasted_iota(jnp.int32, g.shape, 0)
    blk_f = blk.astype(F32)
    valid = blk < own
    g = jnp.where(valid, g, -jnp.inf)
    keep = jnp.zeros(g.shape, F32)
    for _ in range(MOBA_TOPK):
        top = jnp.max(g, axis=0, keepdims=True)
        first = jnp.min(jnp.where(g == top, blk_f, float(n_gate)), axis=0, keepdims=True)
        pick = blk_f == first
        keep = jnp.where(pick, 1.0, keep)
        g = jnp.where(pick, -jnp.inf, g)
    bias = jnp.where(jnp.logical_and(valid, keep > 0.5), 0.0, NEG)
    zeros = lambda rows: jnp.zeros((rows, g.shape[1]), F32)
    return jnp.concatenate([zeros(HEAD_DIM), bias, zeros(LANES - HEAD_DIM - n_gate)], axis=0).T


def _attn_kernel(q_ref, k_ref, v_ref, o_ref, m_ref, acc_ref, sa_ref, sb_ref, *gmat, moba):
    nh, tq = q_ref.shape[1], q_ref.shape[2]
    qi = pl.program_id(1)
    last_blk = k_ref.shape[2] // MOBA_BLOCK - 1

    def values_t(blk, hd):
        return v_ref[0, blk, hd * VT_ROWS:(hd + 1) * VT_ROWS, :]

    def scores_into(dst_ref, blk, queries_ref):
        start = pl.multiple_of(jnp.minimum(blk, last_blk) * MOBA_BLOCK, MOBA_BLOCK)
        for hd in range(nh):
            dst_ref[hd] = lax.dot_general(k_ref[0, hd, pl.ds(start, MOBA_BLOCK), :], queries_ref[hd], _NT,
                                          preferred_element_type=F32)

    q_own_ref = q_ref.at[0]
    if moba:
        gmat_ref, q_past_ref = gmat

        @pl.when(qi == 0)
        def _():
            for hd in range(nh):
                _block_means(k_ref, hd, gmat_ref)

    scores_into(sb_ref, qi, q_own_ref)
    if moba:
        for hd in range(nh):
            gate_t = lax.dot_general(gmat_ref[hd].astype(BF16), q_own_ref[hd], _NT, preferred_element_type=F32)
            q_past_ref[hd] = (q_own_ref[hd].astype(F32) + _select_bias_t(gate_t, qi)).astype(BF16)
    else:
        q_past_ref = q_own_ref
    scores_into(sa_ref, 0, q_past_ref)
    causal = (lax.broadcasted_iota(jnp.int32, (MOBA_BLOCK, tq), 0)
              <= lax.broadcasted_iota(jnp.int32, (MOBA_BLOCK, tq), 1))
    for hd in range(nh):
        m = jnp.max(jnp.where(causal, sb_ref[hd], NEG), axis=0, keepdims=True)
        p = jnp.exp2(jnp.where(causal, sb_ref[hd], NEG) - m)
        m_ref[hd] = m
        acc_ref[hd] = jnp.dot(values_t(qi, hd), p.astype(BF16), preferred_element_type=F32)

    def consume(src_ref, blk, guarded):
        off = jnp.where(blk < qi, 0.0, -NEG) if guarded else None
        vblk = jnp.minimum(blk, last_blk)
        for hd in range(nh):
            m = m_ref[hd]
            smax = jnp.max(src_ref[hd], axis=0, keepdims=True)
            m_new = jnp.maximum(m, smax - off if guarded else smax)
            p = jnp.exp2(src_ref[hd] - (m_new + off if guarded else m_new))
            m_ref[hd] = m_new
            acc_ref[hd] = jnp.exp2(m - m_new) * acc_ref[hd] + jnp.dot(
                values_t(vblk, hd), p.astype(BF16), preferred_element_type=F32)

    def block_pair(t, carry):
        scores_into(sb_ref, 2 * t + 1, q_past_ref)
        consume(sa_ref, 2 * t, False)
        scores_into(sa_ref, 2 * t + 2, q_past_ref)
        consume(sb_ref, 2 * t + 1, True)
        return carry

    lax.fori_loop(0, lax.shift_right_logical(qi + 1, 1), block_pair, 0)
    heads = []
    for hd in range(nh):
        acc = acc_ref[hd]
        heads.append(acc[:HEAD_DIM] * (1.0 / acc[HEAD_DIM:HEAD_DIM + 1]))
    o_ref[0] = jnp.concatenate(heads, axis=0).T.astype(BF16)


def _attention(q, k, vt, moba):
    b, nh, s, _ = q.shape
    tq = ATTN_TILE
    k_spec = pl.BlockSpec((1, nh, s, LANES), lambda bi, i: (bi, 0, 0, 0), pipeline_mode=pl.Buffered(1))
    v_spec = pl.BlockSpec((1,) + vt.shape[1:], lambda bi, i: (bi, 0, 0, 0), pipeline_mode=pl.Buffered(1))
    return pl.pallas_call(
        functools.partial(_attn_kernel, moba=moba),
        grid=(b, s // tq),
        in_specs=[pl.BlockSpec((1, nh, tq, LANES), lambda bi, i: (bi, 0, i, 0)), k_spec, v_spec],
        out_specs=pl.BlockSpec((1, tq, nh * HEAD_DIM), lambda bi, i: (bi, i, 0)),
        out_shape=jax.ShapeDtypeStruct((b, s, nh * HEAD_DIM), BF16),
        scratch_shapes=[pltpu.VMEM((nh, 1, tq), F32), pltpu.VMEM((nh, VT_ROWS, tq), F32),
                        pltpu.VMEM((nh, MOBA_BLOCK, tq), F32), pltpu.VMEM((nh, MOBA_BLOCK, tq), F32)]
        + ([pltpu.VMEM((nh, LANES, LANES), F32), pltpu.VMEM((nh, tq, LANES), BF16)] if moba else []),
        compiler_params=pltpu.CompilerParams(
            dimension_semantics=("arbitrary", "arbitrary"), vmem_limit_bytes=VMEM_LIMIT),
        name="moba_attn" if moba else "mla_attn",
    )(q, k, vt)


def _mix_out_kernel(x_ref, ya_ref, yb_ref, yc_ref, yd_ref, wo_ref, gpost_ref, gpre_ref, gfpost_ref,
                    wup_ref, cw_ref, wdn_ref, o_ref, ext_ref, carry_ref, act_ref):
    tm = x_ref.shape[1]
    first = pl.program_id(1) == 0
    mixed = jnp.concatenate([ya_ref[0], yb_ref[0], yc_ref[0], yd_ref[0]], axis=1)
    y = jnp.dot(mixed, wo_ref[...], preferred_element_type=F32)
    x1 = x_ref[0] + _rms(y, gpost_ref[...])
    hn = _rms(x1, gpre_ref[...]).astype(BF16)

    @pl.when(first)
    def _():
        carry_ref[...] = jnp.zeros(carry_ref.shape, F32)

    fc = FFN_CHUNK
    n_chunks = D_FF // fc

    def up_project(c):
        for part in range(2):
            c0 = part * D_FF + c * fc
            ext = ext_ref.at[c % 2, part]
            ext[0:SUBLANES, :] = carry_ref[:, c0:c0 + fc]
            ext[SUBLANES:SUBLANES + tm, :] = jnp.dot(hn, wup_ref[:, c0:c0 + fc], preferred_element_type=F32)
            carry_ref[:, c0:c0 + fc] = ext[tm:tm + SUBLANES, :]

    def conv3(c, part):
        c0 = part * D_FF + c * fc
        ext = ext_ref.at[c % 2, part]
        conv = cw_ref[0:1, c0:c0 + fc] * ext[SUBLANES - 2:SUBLANES - 2 + tm, :]
        conv += cw_ref[1:2, c0:c0 + fc] * ext[SUBLANES - 1:SUBLANES - 1 + tm, :]
        return conv + cw_ref[2:3, c0:c0 + fc] * ext[SUBLANES:SUBLANES + tm, :]

    up_project(0)
    f = None
    done = 0
    for c in range(n_chunks):
        if c + 1 < n_chunks:
            up_project(c + 1)
        gate = conv3(c, 0)
        act_ref[:, c * fc:(c + 1) * fc] = (gate * _sigmoid(gate) * conv3(c, 1)).astype(BF16)
        if (c + 1) % DOWN_CHUNKS == 0 or c + 1 == n_chunks:
            part = jnp.dot(act_ref[:, done * fc:(c + 1) * fc], wdn_ref[done * fc:(c + 1) * fc, :],
                           preferred_element_type=F32)
            f = part if f is None else f + part
            done = c + 1
    o_ref[0] = x1 + _rms(f, gfpost_ref[...])


def _mix_out(layer, x, ya, yb, yc, yd, wo, gpost, gpre, gfpost, wup, cw, wdn):
    b, s, _ = x.shape
    params = (wo, gpost, gpre, gfpost, wup, cw, wdn)
    tm = TOKEN_TILE
    x_spec = pl.BlockSpec((1, tm, D_MODEL), lambda bi, i: (bi, i, 0))
    group_spec = pl.BlockSpec((1, tm, GROUP_WIDTH), lambda bi, i: (bi, i, 0))
    return pl.pallas_call(
        _mix_out_kernel,
        grid=(b, s // tm),
        in_specs=[x_spec, group_spec, group_spec, group_spec, group_spec]
        + [_layer_spec(p, layer) for p in params],
        out_specs=x_spec,
        out_shape=jax.ShapeDtypeStruct(x.shape, F32),
        scratch_shapes=[pltpu.VMEM((2, 2, tm + SUBLANES, FFN_CHUNK), F32),
                        pltpu.VMEM((SUBLANES, 2 * D_FF), F32),
                        pltpu.VMEM((tm, D_FF), BF16)],
        compiler_params=pltpu.CompilerParams(
            dimension_semantics=("arbitrary", "arbitrary"), vmem_limit_bytes=VMEM_LIMIT),
        name="mix_out",
    )(x, ya, yb, yc, yd, wo, gpost, gpre, gfpost, wup, cw, wdn)


def _pad_heads(w, widths):
    per_head = sum(widths)
    w = w.reshape(w.shape[:-1] + (N_HEADS, per_head))
    w = jnp.pad(w, [(0, 0)] * (w.ndim - 1) + [(0, LANES - per_head)])
    return w.reshape(w.shape[:-2] + (N_HEADS * LANES,))


def _rope_tables(seq, dim, first_lane):
    half = dim // 2
    rel = np.arange(LANES) - first_lane
    rotary = (rel >= 0) & (rel < dim)
    inv = 1.0 / (ROPE_THETA ** (jnp.arange(0, dim, 2, dtype=F32) / dim))
    inv_lane = jnp.where(rotary, inv[np.where(rotary, rel % half, 0)], 0.0)
    ang = jnp.arange(seq, dtype=F32)[:, None] * inv_lane[None, :]
    sin = jnp.sin(ang)
    return jnp.stack([jnp.cos(ang),
                      jnp.where(rotary & (rel < half), -sin, 0.0),
                      jnp.where(rotary & (rel >= half), sin, 0.0)])


def _prep_w_in(w):
    sc, cf = w[..., :768], w[..., 768:1280]
    mq, mk, mv = w[..., 1280:1536], w[..., 1536:1792], w[..., 1792:2048]
    cq, ckv, kr = w[..., 2048:2432], w[..., 2432:2560], w[..., 2560:2592]
    kr = jnp.pad(kr, ((0, 0), (0, 0), (MLA_NOPE_DIM, LANES - MLA_NOPE_DIM - MLA_ROPE_DIM)))
    parts = [sc, cf, _pad_heads(mq, (HEAD_DIM,)), _pad_heads(mk, (HEAD_DIM,)), mv, cq, ckv, kr]
    return jnp.concatenate(parts, axis=-1).astype(BF16)


def _prep_w_ukv(w):
    w = w.reshape(w.shape[:-1] + (N_HEADS, 2, HEAD_DIM))
    k = _pad_heads(w[..., 0, :].reshape(w.shape[:2] + (-1,)), (HEAD_DIM,))
    v = w[..., 1, :].reshape(w.shape[:2] + (-1,))
    return jnp.concatenate([k, v], axis=-1).astype(BF16)


def kernel(x, norm_mix_pre, norm_mix_post, norm_ffn_pre, norm_ffn_post, w_in, conv_a, conv_b, conv_b_bias,
           ln_b_gain, ln_b_bias, mla_q_norm, mla_kv_norm, w_uq, w_ukv, w_out, w_ffn_up, conv_ffn, w_ffn_down):
    b, s, d = x.shape
    depth = w_in.shape[0]
    assert d == D_MODEL and s % TOKEN_TILE == 0 and s // MOBA_BLOCK <= MOBA_BLOCK // SUBLANES
    tc = _rope_tables(s, ROPE_DIM, 0)
    td = _rope_tables(s, MLA_ROPE_DIM, MLA_NOPE_DIM)
    blk = jnp.arange(s, dtype=jnp.int32)[:, None] // MOBA_BLOCK + HEAD_DIM
    oh = (blk == jnp.arange(LANES, dtype=jnp.int32)[None, :]).astype(F32)
    rows = lambda v: v.reshape(depth, 1, -1)
    in_params = (rows(norm_mix_pre), _prep_w_in(w_in), _pad_heads(w_uq, (MLA_NOPE_DIM, MLA_ROPE_DIM)).astype(BF16),
                 _prep_w_ukv(w_ukv), rows(mla_q_norm), rows(mla_kv_norm), conv_a, conv_b, rows(conv_b_bias),
                 rows(ln_b_gain), rows(ln_b_bias))
    out_params = (w_out.astype(BF16), rows(norm_mix_post), rows(norm_ffn_pre), rows(norm_ffn_post),
                  w_ffn_up.astype(BF16), conv_ffn, w_ffn_down.astype(BF16))
    for layer in range(depth):
        ya, yb, qc, kc, vc, qd, kd, vd = _mix_in(layer, x, *in_params, tc, td, oh)
        yc = _attention(qc, kc, vc, True)
        yd = _attention(qd, kd, vd, False)
        x = _mix_out(layer, x, ya, yb, yc, yd, *out_params)
    return x
```

```python
import functools

import jax
import jax.numpy as jnp
import numpy as np
from jax import lax
from jax.experimental import pallas as pl
from jax.experimental.pallas import tpu as pltpu

F32 = jnp.float32
BF16 = jnp.bfloat16

D_MODEL = 1024
GROUP_WIDTH = 256
N_HEADS = 4
HEAD_DIM = 64
SC_KERNEL = 3
CF_KERNEL = 31
FFN_KERNEL = 3
MOBA_BLOCK = 256
MOBA_TOPK = 3
ROPE_THETA = 500000.0
ROPE_DIM = 16
MLA_Q_RANK = 384
MLA_KV_RANK = 128
MLA_NOPE_DIM = 64
MLA_ROPE_DIM = 32
D_FF = 2816
EPS = 1e-6
NEG = -1e30
LOG2E = 1.4426950408889634

LANES = 128
SUBLANES = 8
VMEM_LIMIT = 56 * 1024 * 1024

TOKEN_TILE = 512
ATTN_TILE = 2 * MOBA_BLOCK
CONV_ROWS = 64
VT_ROWS = 80
CF_HALO = 32
FFN_CHUNK = 256
DOWN_CHUNKS = 6

_C_SC = 0
_C_CF = _C_SC + 3 * GROUP_WIDTH
_C_MQ = _C_CF + 2 * GROUP_WIDTH
_C_MK = _C_MQ + N_HEADS * LANES
_C_MV = _C_MK + N_HEADS * LANES
_C_CQ = _C_MV + GROUP_WIDTH
_C_CKV = _C_CQ + MLA_Q_RANK
_C_KR = _C_CKV + MLA_KV_RANK
_C_END = _C_KR + LANES


def _layer_spec(stacked, layer):
    zeros = (0,) * (stacked.ndim - 1)
    return pl.BlockSpec((None,) + stacked.shape[1:], lambda *_: (layer,) + zeros, pipeline_mode=pl.Buffered(1))


def _rms(x, g):
    return x * lax.rsqrt(jnp.mean(x * x, axis=-1, keepdims=True) + EPS) * g


def _sigmoid(x):
    return 1.0 / (1.0 + jnp.exp(-x))


def _rope(x, tab_ref, shift):
    return (x * tab_ref[0]
            + pltpu.roll(x, LANES - shift, 1) * tab_ref[1]
            + pltpu.roll(x, shift, 1) * tab_ref[2])


def _carry_halo(ext_ref, halo, rows, first):
    @pl.when(first)
    def _():
        ext_ref[0:halo, :] = jnp.zeros((halo, ext_ref.shape[1]), F32)

    @pl.when(jnp.logical_not(first))
    def _():
        ext_ref[0:halo, :] = ext_ref[rows:rows + halo, :]


def _store_values_t(vt_ref, v):
    ones = jnp.ones((VT_ROWS - HEAD_DIM, MOBA_BLOCK), BF16)
    for blk in range(v.shape[0] // MOBA_BLOCK):
        vt = v[blk * MOBA_BLOCK:(blk + 1) * MOBA_BLOCK, :].T.astype(BF16)
        for hd in range(N_HEADS):
            vt_ref[0, blk, hd * VT_ROWS:hd * VT_ROWS + HEAD_DIM, :] = vt[hd * HEAD_DIM:(hd + 1) * HEAD_DIM, :]
            vt_ref[0, blk, hd * VT_ROWS + HEAD_DIM:(hd + 1) * VT_ROWS, :] = ones


def _mix_in_kernel(x_ref, g_ref, w_ref, wuq_ref, wukv_ref, gq_ref, gkv_ref,
                   ca_ref, cb_ref, cbb_ref, lng_ref, lnb_ref, tc_ref, td_ref, oh_ref,
                   ya_ref, yb_ref, qc_ref, kc_ref, vc_ref, qd_ref, kd_ref, vd_ref,
                   exta_ref, extb_ref, zatt_ref, shift_ref):
    tm = x_ref.shape[1]
    first = pl.program_id(1) == 0
    h = _rms(x_ref[0], g_ref[...]).astype(BF16)

    def proj(a, b):
        return jnp.dot(h, w_ref[:, a:b], preferred_element_type=F32)

    z = proj(_C_SC, _C_CF)
    _carry_halo(exta_ref, SUBLANES, tm, first)
    exta_ref[SUBLANES:SUBLANES + tm, :] = z[:, 2 * GROUP_WIDTH:] * z[:, :GROUP_WIDTH]
    conv = ca_ref[0:1, :] * exta_ref[SUBLANES - 2:SUBLANES - 2 + tm, :]
    conv += ca_ref[1:2, :] * exta_ref[SUBLANES - 1:SUBLANES - 1 + tm, :]
    conv += ca_ref[2:3, :] * exta_ref[SUBLANES:SUBLANES + tm, :]
    ya_ref[0] = (z[:, GROUP_WIDTH:2 * GROUP_WIDTH] * conv).astype(BF16)

    z = proj(_C_CF, _C_MQ)
    _carry_halo(extb_ref, CF_HALO, tm, first)
    extb_ref[CF_HALO:CF_HALO + tm, :] = z[:, :GROUP_WIDTH] * _sigmoid(z[:, GROUP_WIDTH:])
    zatt_ref[...] = proj(_C_MQ, _C_END)
    zq = zatt_ref.at[:, 0:_C_MK - _C_MQ]
    zk = zatt_ref.at[:, _C_MK - _C_MQ:_C_MV - _C_MQ]
    zv = zatt_ref.at[:, _C_MV - _C_MQ:_C_CQ - _C_MQ]
    zd = zatt_ref.at[:, _C_CQ - _C_MQ:_C_END - _C_MQ]
    span = tm + CF_HALO - SUBLANES
    for j in range(1, SUBLANES):
        shift_ref[j - 1, 0:span, :] = extb_ref[j:j + span, :]
    for r in range(tm // CONV_ROWS):
        base = r * CONV_ROWS + CF_HALO - (CF_KERNEL - 1)
        acc = jnp.broadcast_to(cbb_ref[...], (CONV_ROWS, GROUP_WIDTH))
        for k in range(CF_KERNEL):
            offset = (base + k) % SUBLANES
            rows = extb_ref if offset == 0 else shift_ref.at[offset - 1]
            acc = acc + cb_ref[k:k + 1, :] * rows[base + k - offset:base + k - offset + CONV_ROWS, :]
        xc = acc - jnp.mean(acc, axis=-1, keepdims=True)
        y = xc * lax.rsqrt(jnp.mean(xc * xc, axis=-1, keepdims=True) + EPS) * lng_ref[...] + lnb_ref[...]
        yb_ref[0, r * CONV_ROWS:(r + 1) * CONV_ROWS, :] = (y * _sigmoid(y)).astype(BF16)

    scale_c = HEAD_DIM ** -0.5 * LOG2E
    for hd in range(N_HEADS):
        sl = slice(hd * LANES, (hd + 1) * LANES)
        qc_ref[0, hd] = (_rope(zq[:, sl], tc_ref, ROPE_DIM // 2) * scale_c).astype(BF16)
        kc_ref[0, hd] = (_rope(zk[:, sl], tc_ref, ROPE_DIM // 2) + oh_ref[...]).astype(BF16)
    _store_values_t(vc_ref, zv[...])

    cq = _rms(zd[:, :MLA_Q_RANK], gq_ref[...]).astype(BF16)
    ckv = _rms(zd[:, MLA_Q_RANK:MLA_Q_RANK + MLA_KV_RANK], gkv_ref[...]).astype(BF16)
    kr = _rope(zd[:, MLA_Q_RANK + MLA_KV_RANK:], td_ref, MLA_ROPE_DIM // 2)
    qd = jnp.dot(cq, wuq_ref[...], preferred_element_type=F32)
    kvd = jnp.dot(ckv, wukv_ref[...], preferred_element_type=F32)
    scale_d = (MLA_NOPE_DIM + MLA_ROPE_DIM) ** -0.5 * LOG2E
    for hd in range(N_HEADS):
        sl = slice(hd * LANES, (hd + 1) * LANES)
        qd_ref[0, hd] = (_rope(qd[:, sl], td_ref, MLA_ROPE_DIM // 2) * scale_d).astype(BF16)
        kd_ref[0, hd] = (kvd[:, sl] + kr).astype(BF16)
    _store_values_t(vd_ref, kvd[:, N_HEADS * LANES:])


def _mix_in(layer, x, g, w, wuq, wukv, gq, gkv, ca, cb, cbb, lng, lnb, tc, td, oh):
    b, s, _ = x.shape
    params = (g, w, wuq, wukv, gq, gkv, ca, cb, cbb, lng, lnb)
    tm = TOKEN_TILE
    head_shape = jax.ShapeDtypeStruct((b, N_HEADS, s, LANES), BF16)
    head_spec = pl.BlockSpec((1, N_HEADS, tm, LANES), lambda bi, i: (bi, 0, i, 0))
    group_shape = jax.ShapeDtypeStruct((b, s, GROUP_WIDTH), BF16)
    group_spec = pl.BlockSpec((1, tm, GROUP_WIDTH), lambda bi, i: (bi, i, 0))
    tab_spec = pl.BlockSpec((3, tm, LANES), lambda bi, i: (0, i, 0))
    vt_shape = jax.ShapeDtypeStruct((b, s // MOBA_BLOCK, N_HEADS * VT_ROWS, MOBA_BLOCK), BF16)
    vt_spec = pl.BlockSpec((1, tm // MOBA_BLOCK, N_HEADS * VT_ROWS, MOBA_BLOCK), lambda bi, i: (bi, i, 0, 0))
    return pl.pallas_call(
        _mix_in_kernel,
        grid=(b, s // tm),
        in_specs=[pl.BlockSpec((1, tm, D_MODEL), lambda bi, i: (bi, i, 0))]
        + [_layer_spec(p, layer) for p in params]
        + [tab_spec, tab_spec, pl.BlockSpec((tm, LANES), lambda bi, i: (i, 0))],
        out_specs=[group_spec, group_spec] + [head_spec, head_spec, vt_spec] * 2,
        out_shape=[group_shape, group_shape] + [head_shape, head_shape, vt_shape] * 2,
        scratch_shapes=[pltpu.VMEM((tm + SUBLANES, GROUP_WIDTH), F32),
                        pltpu.VMEM((tm + CF_HALO, GROUP_WIDTH), F32),
                        pltpu.VMEM((tm, _C_END - _C_MQ), F32),
                        pltpu.VMEM((SUBLANES - 1, tm + CF_HALO, GROUP_WIDTH), F32)],
        compiler_params=pltpu.CompilerParams(
            dimension_semantics=("arbitrary", "arbitrary"), vmem_limit_bytes=VMEM_LIMIT),
        name="mix_in",
    )(x, g, w, wuq, wukv, gq, gkv, ca, cb, cbb, lng, lnb, tc, td, oh)


_NT = (((1,), (1,)), ((), ()))


def _block_means(k_ref, hd, gmat_ref):
    nblk = k_ref.shape[2] // MOBA_BLOCK
    keep = (lax.broadcasted_iota(jnp.int32, (1, LANES), 1) < HEAD_DIM).astype(F32)
    gmat_ref[hd] = jnp.zeros((LANES, LANES), F32)
    for n in range(nblk):
        blk = k_ref[0, hd, n * MOBA_BLOCK:(n + 1) * MOBA_BLOCK, :].astype(F32)
        gmat_ref[hd, HEAD_DIM + n:HEAD_DIM + n + 1, :] = jnp.mean(blk, axis=0, keepdims=True) * keep


def _select_bias_t(gate_t, own):
    n_gate = MOBA_BLOCK // SUBLANES
    g = gate_t[HEAD_DIM:HEAD_DIM + n_gate, :]
    blk = lax.broadcasted_iota(jnp.int32, g.shape, 0)
    blk_f = blk.astype(F32)
    valid = blk < own
    g = jnp.where(valid, g, -jnp.inf)
    keep = jnp.zeros(g.shape, F32)
    for _ in range(MOBA_TOPK):
        top = jnp.max(g, axis=0, keepdims=True)
        first = jnp.min(jnp.where(g == top, blk_f, float(n_gate)), axis=0, keepdims=True)
        pick = blk_f == first
        keep = jnp.where(pick, 1.0, keep)
        g = jnp.where(pick, -jnp.inf, g)
    bias = jnp.where(jnp.logical_and(valid, keep > 0.5), 0.0, NEG)
    zeros = lambda rows: jnp.zeros((rows, g.shape[1]), F32)
    return jnp.concatenate([zeros(HEAD_DIM), bias, zeros(LANES - HEAD_DIM - n_gate)], axis=0).T


def _attn_kernel(q_ref, k_ref, v_ref, o_ref, m_ref, acc_ref, sa_ref, sb_ref, sc_ref, *gmat, moba):
    nh, tq = q_ref.shape[1], q_ref.shape[2]
    assert tq == 2 * MOBA_BLOCK
    qi = pl.program_id(1)
    first_own = 2 * qi
    last_blk = k_ref.shape[2] // MOBA_BLOCK - 1

    def values_t(blk, hd):
        return v_ref[0, blk, hd * VT_ROWS:(hd + 1) * VT_ROWS, :]

    def scores_into(dst_ref, blk, queries_ref):
        start = pl.multiple_of(jnp.minimum(blk, last_blk) * MOBA_BLOCK, MOBA_BLOCK)
        for hd in range(nh):
            dst_ref[hd] = lax.dot_general(k_ref[0, hd, pl.ds(start, MOBA_BLOCK), :], queries_ref[hd], _NT,
                                          preferred_element_type=F32)

    def consume(src_ref, vblk, mask=None, first=False):
        def scores(hd):
            return src_ref[hd] if mask is None else jnp.where(mask, src_ref[hd], NEG)

        for hd in range(nh):
            smax = jnp.max(scores(hd), axis=0, keepdims=True)
            m_new = smax if first else jnp.maximum(m_ref[hd], smax)
            pv = jnp.dot(values_t(vblk, hd), jnp.exp2(scores(hd) - m_new).astype(BF16),
                         preferred_element_type=F32)
            acc_ref[hd] = pv if first else jnp.exp2(m_ref[hd] - m_new) * acc_ref[hd] + pv
            m_ref[hd] = m_new

    q_own_ref = q_ref.at[0]
    if moba:
        gmat_ref, q_past_ref, q_mix_ref = gmat

        @pl.when(qi == 0)
        def _():
            for hd in range(nh):
                _block_means(k_ref, hd, gmat_ref)

        query = lax.broadcasted_iota(jnp.int32, (1, tq), 1)
        own = first_own + jnp.where(query >= MOBA_BLOCK, 1, 0)
        for hd in range(nh):
            gate_t = lax.dot_general(gmat_ref[hd].astype(BF16), q_own_ref[hd], _NT, preferred_element_type=F32)
            q_past_ref[hd] = (q_own_ref[hd].astype(F32) + _select_bias_t(gate_t, own)).astype(BF16)
            q_mix_ref[hd, 0:MOBA_BLOCK, :] = q_own_ref[hd, 0:MOBA_BLOCK, :]
            q_mix_ref[hd, MOBA_BLOCK:tq, :] = q_past_ref[hd, MOBA_BLOCK:tq, :]
    else:
        q_past_ref = q_mix_ref = q_own_ref

    scores_into(sb_ref, first_own, q_mix_ref)
    scores_into(sc_ref, first_own + 1, q_own_ref)
    scores_into(sa_ref, 0, q_past_ref)
    key = lax.broadcasted_iota(jnp.int32, (MOBA_BLOCK, tq), 0)
    qry = lax.broadcasted_iota(jnp.int32, (MOBA_BLOCK, tq), 1)
    consume(sb_ref, first_own, mask=key <= qry, first=True)
    consume(sc_ref, first_own + 1, mask=key + MOBA_BLOCK <= qry)

    def block_pair(t, carry):
        scores_into(sb_ref, 2 * t + 1, q_past_ref)
        consume(sa_ref, 2 * t)
        scores_into(sa_ref, 2 * t + 2, q_past_ref)
        consume(sb_ref, 2 * t + 1)
        return carry

    lax.fori_loop(0, qi, block_pair, 0)
    heads = []
    for hd in range(nh):
        acc = acc_ref[hd]
        heads.append(acc[:HEAD_DIM] * (1.0 / acc[HEAD_DIM:HEAD_DIM + 1]))
    o_ref[0] = jnp.concatenate(heads, axis=0).T.astype(BF16)


def _attention(q, k, vt, moba):
    b, nh, s, _ = q.shape
    tq = ATTN_TILE
    k_spec = pl.BlockSpec((1, nh, s, LANES), lambda bi, i: (bi, 0, 0, 0), pipeline_mode=pl.Buffered(1))
    v_spec = pl.BlockSpec((1,) + vt.shape[1:], lambda bi, i: (bi, 0, 0, 0), pipeline_mode=pl.Buffered(1))
    scores = pltpu.VMEM((nh, MOBA_BLOCK, tq), F32)
    queries = pltpu.VMEM((nh, tq, LANES), BF16)
    return pl.pallas_call(
        functools.partial(_attn_kernel, moba=moba),
        grid=(b, s // tq),
        in_specs=[pl.BlockSpec((1, nh, tq, LANES), lambda bi, i: (bi, 0, i, 0)), k_spec, v_spec],
        out_specs=pl.BlockSpec((1, tq, nh * HEAD_DIM), lambda bi, i: (bi, i, 0)),
        out_shape=jax.ShapeDtypeStruct((b, s, nh * HEAD_DIM), BF16),
        scratch_shapes=[pltpu.VMEM((nh, 1, tq), F32), pltpu.VMEM((nh, VT_ROWS, tq), F32), scores, scores, scores]
        + ([pltpu.VMEM((nh, LANES, LANES), F32), queries, queries] if moba else []),
        compiler_params=pltpu.CompilerParams(
            dimension_semantics=("arbitrary", "arbitrary"), vmem_limit_bytes=VMEM_LIMIT),
        name="moba_attn" if moba else "mla_attn",
    )(q, k, vt)


def _mix_out_kernel(x_ref, ya_ref, yb_ref, yc_ref, yd_ref, wo_ref, gpost_ref, gpre_ref, gfpost_ref,
                    wup_ref, cw_ref, wdn_ref, o_ref, ext_ref, carry_ref, act_ref):
    tm = x_ref.shape[1]
    first = pl.program_id(1) == 0
    mixed = jnp.concatenate([ya_ref[0], yb_ref[0], yc_ref[0], yd_ref[0]], axis=1)
    y = jnp.dot(mixed, wo_ref[...], preferred_element_type=F32)
    x1 = x_ref[0] + _rms(y, gpost_ref[...])
    hn = _rms(x1, gpre_ref[...]).astype(BF16)

    @pl.when(first)
    def _():
        carry_ref[...] = jnp.zeros(carry_ref.shape, F32)

    fc = FFN_CHUNK
    n_chunks = D_FF // fc

    def up_project(c):
        for part in range(2):
            c0 = part * D_FF + c * fc
            ext = ext_ref.at[c % 2, part]
            ext[0:SUBLANES, :] = carry_ref[:, c0:c0 + fc]
            ext[SUBLANES:SUBLANES + tm, :] = jnp.dot(hn, wup_ref[:, c0:c0 + fc], preferred_element_type=F32)
            carry_ref[:, c0:c0 + fc] = ext[tm:tm + SUBLANES, :]

    def conv3(c, part):
        c0 = part * D_FF + c * fc
        ext = ext_ref.at[c % 2, part]
        conv = cw_ref[0:1, c0:c0 + fc] * ext[SUBLANES - 2:SUBLANES - 2 + tm, :]
        conv += cw_ref[1:2, c0:c0 + fc] * ext[SUBLANES - 1:SUBLANES - 1 + tm, :]
        return conv + cw_ref[2:3, c0:c0 + fc] * ext[SUBLANES:SUBLANES + tm, :]

    up_project(0)
    f = None
    done = 0
    for c in range(n_chunks):
        if c + 1 < n_chunks:
            up_project(c + 1)
        gate = conv3(c, 0)
        act_ref[:, c * fc:(c + 1) * fc] = (gate * _sigmoid(gate) * conv3(c, 1)).astype(BF16)
        if (c + 1) % DOWN_CHUNKS == 0 or c + 1 == n_chunks:
            part = jnp.dot(act_ref[:, done * fc:(c + 1) * fc], wdn_ref[done * fc:(c + 1) * fc, :],
                           preferred_element_type=F32)
            f = part if f is None else f + part
            done = c + 1
    o_ref[0] = x1 + _rms(f, gfpost_ref[...])


def _mix_out(layer, x, ya, yb, yc, yd, wo, gpost, gpre, gfpost, wup, cw, wdn):
    b, s, _ = x.shape
    params = (wo, gpost, gpre, gfpost, wup, cw, wdn)
    tm = TOKEN_TILE
    x_spec = pl.BlockSpec((1, tm, D_MODEL), lambda bi, i: (bi, i, 0))
    group_spec = pl.BlockSpec((1, tm, GROUP_WIDTH), lambda bi, i: (bi, i, 0))
    return pl.pallas_call(
        _mix_out_kernel,
        grid=(b, s // tm),
        in_specs=[x_spec, group_spec, group_spec, group_spec, group_spec]
        + [_layer_spec(p, layer) for p in params],
        out_specs=x_spec,
        out_shape=jax.ShapeDtypeStruct(x.shape, F32),
        scratch_shapes=[pltpu.VMEM((2, 2, tm + SUBLANES, FFN_CHUNK), F32),
                        pltpu.VMEM((SUBLANES, 2 * D_FF), F32),
                        pltpu.VMEM((tm, D_FF), BF16)],
        compiler_params=pltpu.CompilerParams(
            dimension_semantics=("arbitrary", "arbitrary"), vmem_limit_bytes=VMEM_LIMIT),
        name="mix_out",
    )(x, ya, yb, yc, yd, wo, gpost, gpre, gfpost, wup, cw, wdn)


def _pad_heads(w, widths):
    per_head = sum(widths)
    w = w.reshape(w.shape[:-1] + (N_HEADS, per_head))
    w = jnp.pad(w, [(0, 0)] * (w.ndim - 1) + [(0, LANES - per_head)])
    return w.reshape(w.shape[:-2] + (N_HEADS * LANES,))


def _rope_tables(seq, dim, first_lane):
    half = dim // 2
    rel = np.arange(LANES) - first_lane
    rotary = (rel >= 0) & (rel < dim)
    inv = 1.0 / (ROPE_THETA ** (jnp.arange(0, dim, 2, dtype=F32) / dim))
    inv_lane = jnp.where(rotary, inv[np.where(rotary, rel % half, 0)], 0.0)
    ang = jnp.arange(seq, dtype=F32)[:, None] * inv_lane[None, :]
    sin = jnp.sin(ang)
    return jnp.stack([jnp.cos(ang),
                      jnp.where(rotary & (rel < half), -sin, 0.0),
                      jnp.where(rotary & (rel >= half), sin, 0.0)])


def _prep_w_in(w):
    sc, cf = w[..., :768], w[..., 768:1280]
    mq, mk, mv = w[..., 1280:1536], w[..., 1536:1792], w[..., 1792:2048]
    cq, ckv, kr = w[..., 2048:2432], w[..., 2432:2560], w[..., 2560:2592]
    kr = jnp.pad(kr, ((0, 0), (0, 0), (MLA_NOPE_DIM, LANES - MLA_NOPE_DIM - MLA_ROPE_DIM)))
    parts = [sc, cf, _pad_heads(mq, (HEAD_DIM,)), _pad_heads(mk, (HEAD_DIM,)), mv, cq, ckv, kr]
    return jnp.concatenate(parts, axis=-1).astype(BF16)


def _prep_w_ukv(w):
    w = w.reshape(w.shape[:-1] + (N_HEADS, 2, HEAD_DIM))
    k = _pad_heads(w[..., 0, :].reshape(w.shape[:2] + (-1,)), (HEAD_DIM,))
    v = w[..., 1, :].reshape(w.shape[:2] + (-1,))
    return jnp.concatenate([k, v], axis=-1).astype(BF16)


def kernel(x, norm_mix_pre, norm_mix_post, norm_ffn_pre, norm_ffn_post, w_in, conv_a, conv_b, conv_b_bias,
           ln_b_gain, ln_b_bias, mla_q_norm, mla_kv_norm, w_uq, w_ukv, w_out, w_ffn_up, conv_ffn, w_ffn_down):
    b, s, d = x.shape
    depth = w_in.shape[0]
    assert d == D_MODEL and s % TOKEN_TILE == 0 and s // MOBA_BLOCK <= MOBA_BLOCK // SUBLANES
    tc = _rope_tables(s, ROPE_DIM, 0)
    td = _rope_tables(s, MLA_ROPE_DIM, MLA_NOPE_DIM)
    blk = jnp.arange(s, dtype=jnp.int32)[:, None] // MOBA_BLOCK + HEAD_DIM
    oh = (blk == jnp.arange(LANES, dtype=jnp.int32)[None, :]).astype(F32)
    rows = lambda v: v.reshape(depth, 1, -1)
    in_params = (rows(norm_mix_pre), _prep_w_in(w_in), _pad_heads(w_uq, (MLA_NOPE_DIM, MLA_ROPE_DIM)).astype(BF16),
                 _prep_w_ukv(w_ukv), rows(mla_q_norm), rows(mla_kv_norm), conv_a, conv_b, rows(conv_b_bias),
                 rows(ln_b_gain), rows(ln_b_bias))
    out_params = (w_out.astype(BF16), rows(norm_mix_post), rows(norm_ffn_pre), rows(norm_ffn_post),
                  w_ffn_up.astype(BF16), conv_ffn, w_ffn_down.astype(BF16))
    for layer in range(depth):
        ya, yb, qc, kc, vc, qd, kd, vd = _mix_in(layer, x, *in_params, tc, td, oh)
        yc = _attention(qc, kc, vc, True)
        yd = _attention(qd, kd, vd, False)
        x = _mix_out(layer, x, ya, yb, yc, yd, *out_params)
    return x
```

```python
import functools

import jax
import jax.numpy as jnp
import numpy as np
from jax import lax
from jax.experimental import pallas as pl
from jax.experimental.pallas import tpu as pltpu

F32 = jnp.float32
BF16 = jnp.bfloat16

D_MODEL = 1024
GROUP_WIDTH = 256
N_HEADS = 4
HEAD_DIM = 64
SC_KERNEL = 3
CF_KERNEL = 31
FFN_KERNEL = 3
MOBA_BLOCK = 256
MOBA_TOPK = 3
ROPE_THETA = 500000.0
ROPE_DIM = 16
MLA_Q_RANK = 384
MLA_KV_RANK = 128
MLA_NOPE_DIM = 64
MLA_ROPE_DIM = 32
D_FF = 2816
EPS = 1e-6
NEG = -1e30
LOG2E = 1.4426950408889634

LANES = 128
SUBLANES = 8
VMEM_LIMIT = 56 * 1024 * 1024

TOKEN_TILE = 512
ATTN_TILE = 2 * MOBA_BLOCK
CONV_ROWS = 64
VT_ROWS = 80
CF_HALO = 32
FFN_CHUNK = 256
DOWN_CHUNKS = 6

_C_SC = 0
_C_CF = _C_SC + 3 * GROUP_WIDTH
_C_MQ = _C_CF + 2 * GROUP_WIDTH
_C_MK = _C_MQ + N_HEADS * LANES
_C_MV = _C_MK + N_HEADS * LANES
_C_CQ = _C_MV + GROUP_WIDTH
_C_CKV = _C_CQ + MLA_Q_RANK
_C_KR = _C_CKV + MLA_KV_RANK
_C_END = _C_KR + LANES


def _layer_spec(stacked, layer):
    zeros = (0,) * (stacked.ndim - 1)
    return pl.BlockSpec((None,) + stacked.shape[1:], lambda *_: (layer,) + zeros, pipeline_mode=pl.Buffered(1))


def _rms(x, g):
    return x * lax.rsqrt(jnp.mean(x * x, axis=-1, keepdims=True) + EPS) * g


def _sigmoid(x):
    return 1.0 / (1.0 + jnp.exp(-x))


def _rope(x, tab_ref, shift):
    return (x * tab_ref[0]
            + pltpu.roll(x, LANES - shift, 1) * tab_ref[1]
            + pltpu.roll(x, shift, 1) * tab_ref[2])


def _carry_halo(ext_ref, halo, rows, first):
    @pl.when(first)
    def _():
        ext_ref[0:halo, :] = jnp.zeros((halo, ext_ref.shape[1]), F32)

    @pl.when(jnp.logical_not(first))
    def _():
        ext_ref[0:halo, :] = ext_ref[rows:rows + halo, :]


def _store_values_t(vt_ref, v):
    ones = jnp.ones((VT_ROWS - HEAD_DIM, MOBA_BLOCK), BF16)
    for blk in range(v.shape[0] // MOBA_BLOCK):
        vt = v[blk * MOBA_BLOCK:(blk + 1) * MOBA_BLOCK, :].T.astype(BF16)
        for hd in range(N_HEADS):
            vt_ref[0, blk, hd * VT_ROWS:hd * VT_ROWS + HEAD_DIM, :] = vt[hd * HEAD_DIM:(hd + 1) * HEAD_DIM, :]
            vt_ref[0, blk, hd * VT_ROWS + HEAD_DIM:(hd + 1) * VT_ROWS, :] = ones


def _mix_in_kernel(x_ref, g_ref, w_ref, wuq_ref, wukv_ref, gq_ref, gkv_ref,
                   ca_ref, cb_ref, cbb_ref, lng_ref, lnb_ref, tc_ref, td_ref, oh_ref,
                   ya_ref, yb_ref, qc_ref, kc_ref, vc_ref, qd_ref, kd_ref, vd_ref,
                   exta_ref, extb_ref, shift_ref):
    tm = x_ref.shape[1]
    first = pl.program_id(1) == 0
    h = _rms(x_ref[0], g_ref[...]).astype(BF16)

    def proj(a, b):
        return jnp.dot(h, w_ref[:, a:b], preferred_element_type=F32)

    z = proj(_C_SC, _C_CF)
    _carry_halo(exta_ref, SUBLANES, tm, first)
    exta_ref[SUBLANES:SUBLANES + tm, :] = z[:, 2 * GROUP_WIDTH:] * z[:, :GROUP_WIDTH]
    conv = ca_ref[0:1, :] * exta_ref[SUBLANES - 2:SUBLANES - 2 + tm, :]
    conv += ca_ref[1:2, :] * exta_ref[SUBLANES - 1:SUBLANES - 1 + tm, :]
    conv += ca_ref[2:3, :] * exta_ref[SUBLANES:SUBLANES + tm, :]
    ya_ref[0] = (z[:, GROUP_WIDTH:2 * GROUP_WIDTH] * conv).astype(BF16)

    z = proj(_C_CF, _C_MQ)
    _carry_halo(extb_ref, CF_HALO, tm, first)
    extb_ref[CF_HALO:CF_HALO + tm, :] = z[:, :GROUP_WIDTH] * _sigmoid(z[:, GROUP_WIDTH:])
    span = tm + CF_HALO - SUBLANES
    for j in range(1, SUBLANES):
        shift_ref[j - 1, 0:span, :] = extb_ref[j:j + span, :]

    def conv_chunk(r):
        base = r * CONV_ROWS + CF_HALO - (CF_KERNEL - 1)
        acc = jnp.broadcast_to(cbb_ref[...], (CONV_ROWS, GROUP_WIDTH))
        for k in range(CF_KERNEL):
            offset = (base + k) % SUBLANES
            rows = extb_ref if offset == 0 else shift_ref.at[offset - 1]
            acc = acc + cb_ref[k:k + 1, :] * rows[base + k - offset:base + k - offset + CONV_ROWS, :]
        xc = acc - jnp.mean(acc, axis=-1, keepdims=True)
        y = xc * lax.rsqrt(jnp.mean(xc * xc, axis=-1, keepdims=True) + EPS) * lng_ref[...] + lnb_ref[...]
        yb_ref[0, r * CONV_ROWS:(r + 1) * CONV_ROWS, :] = (y * _sigmoid(y)).astype(BF16)

    scale_c = HEAD_DIM ** -0.5 * LOG2E

    def moba_q(pair):
        z = proj(_C_MQ + 2 * pair * LANES, _C_MQ + 2 * (pair + 1) * LANES)
        for j in range(2):
            qc_ref[0, 2 * pair + j] = (_rope(z[:, j * LANES:(j + 1) * LANES], tc_ref, ROPE_DIM // 2)
                                       * scale_c).astype(BF16)

    def moba_k(pair):
        z = proj(_C_MK + 2 * pair * LANES, _C_MK + 2 * (pair + 1) * LANES)
        for j in range(2):
            kc_ref[0, 2 * pair + j] = (_rope(z[:, j * LANES:(j + 1) * LANES], tc_ref, ROPE_DIM // 2)
                                       + oh_ref[...]).astype(BF16)

    def moba_v():
        _store_values_t(vc_ref, proj(_C_MV, _C_CQ))

    scale_d = (MLA_NOPE_DIM + MLA_ROPE_DIM) ** -0.5 * LOG2E

    def mla_q():
        cq = _rms(proj(_C_CQ, _C_CKV), gq_ref[...]).astype(BF16)
        qd = jnp.dot(cq, wuq_ref[...], preferred_element_type=F32)
        for hd in range(N_HEADS):
            qd_ref[0, hd] = (_rope(qd[:, hd * LANES:(hd + 1) * LANES], td_ref, MLA_ROPE_DIM // 2)
                             * scale_d).astype(BF16)

    def mla_kv():
        z = proj(_C_CKV, _C_END)
        ckv = _rms(z[:, :MLA_KV_RANK], gkv_ref[...]).astype(BF16)
        kr = _rope(z[:, MLA_KV_RANK:], td_ref, MLA_ROPE_DIM // 2)
        kvd = jnp.dot(ckv, wukv_ref[...], preferred_element_type=F32)
        for hd in range(N_HEADS):
            kd_ref[0, hd] = (kvd[:, hd * LANES:(hd + 1) * LANES] + kr).astype(BF16)
        _store_values_t(vd_ref, kvd[:, N_HEADS * LANES:])

    pieces = [lambda: moba_q(0), lambda: moba_q(1), lambda: moba_k(0), lambda: moba_k(1), moba_v, mla_q, mla_kv]
    for r in range(tm // CONV_ROWS):
        conv_chunk(r)
        if r < len(pieces):
            pieces[r]()
    for piece in pieces[tm // CONV_ROWS:]:
        piece()


def _mix_in(layer, x, g, w, wuq, wukv, gq, gkv, ca, cb, cbb, lng, lnb, tc, td, oh):
    b, s, _ = x.shape
    params = (g, w, wuq, wukv, gq, gkv, ca, cb, cbb, lng, lnb)
    tm = TOKEN_TILE
    head_shape = jax.ShapeDtypeStruct((b, N_HEADS, s, LANES), BF16)
    head_spec = pl.BlockSpec((1, N_HEADS, tm, LANES), lambda bi, i: (bi, 0, i, 0))
    group_shape = jax.ShapeDtypeStruct((b, s, GROUP_WIDTH), BF16)
    group_spec = pl.BlockSpec((1, tm, GROUP_WIDTH), lambda bi, i: (bi, i, 0))
    tab_spec = pl.BlockSpec((3, tm, LANES), lambda bi, i: (0, i, 0))
    vt_shape = jax.ShapeDtypeStruct((b, s // MOBA_BLOCK, N_HEADS * VT_ROWS, MOBA_BLOCK), BF16)
    vt_spec = pl.BlockSpec((1, tm // MOBA_BLOCK, N_HEADS * VT_ROWS, MOBA_BLOCK), lambda bi, i: (bi, i, 0, 0))
    return pl.pallas_call(
        _mix_in_kernel,
        grid=(b, s // tm),
        in_specs=[pl.BlockSpec((1, tm, D_MODEL), lambda bi, i: (bi, i, 0))]
        + [_layer_spec(p, layer) for p in params]
        + [tab_spec, tab_spec, pl.BlockSpec((tm, LANES), lambda bi, i: (i, 0))],
        out_specs=[group_spec, group_spec] + [head_spec, head_spec, vt_spec] * 2,
        out_shape=[group_shape, group_shape] + [head_shape, head_shape, vt_shape] * 2,
        scratch_shapes=[pltpu.VMEM((tm + SUBLANES, GROUP_WIDTH), F32),
                        pltpu.VMEM((tm + CF_HALO, GROUP_WIDTH), F32),
                        pltpu.VMEM((SUBLANES - 1, tm + CF_HALO, GROUP_WIDTH), F32)],
        compiler_params=pltpu.CompilerParams(
            dimension_semantics=("arbitrary", "arbitrary"), vmem_limit_bytes=VMEM_LIMIT),
        name="mix_in",
    )(x, g, w, wuq, wukv, gq, gkv, ca, cb, cbb, lng, lnb, tc, td, oh)


_NT = (((1,), (1,)), ((), ()))


def _block_means(k_ref, hd, gmat_ref):
    nblk = k_ref.shape[2] // MOBA_BLOCK
    keep = (lax.broadcasted_iota(jnp.int32, (1, LANES), 1) < HEAD_DIM).astype(F32)
    gmat_ref[hd] = jnp.zeros((LANES, LANES), F32)
    for n in range(nblk):
        blk = k_ref[0, hd, n * MOBA_BLOCK:(n + 1) * MOBA_BLOCK, :].astype(F32)
        gmat_ref[hd, HEAD_DIM + n:HEAD_DIM + n + 1, :] = jnp.mean(blk, axis=0, keepdims=True) * keep


def _select_bias_t(gate_t, own):
    n_gate = MOBA_BLOCK // SUBLANES
    g = gate_t[HEAD_DIM:HEAD_DIM + n_gate, :]
    blk = lax.broadcasted_iota(jnp.int32, g.shape, 0)
    blk_f = blk.astype(F32)
    valid = blk < own
    g = jnp.where(valid, g, -jnp.inf)
    keep = jnp.zeros(g.shape, F32)
    for _ in range(MOBA_TOPK):
        top = jnp.max(g, axis=0, keepdims=True)
        first = jnp.min(jnp.where(g == top, blk_f, float(n_gate)), axis=0, keepdims=True)
        pick = blk_f == first
        keep = jnp.where(pick, 1.0, keep)
        g = jnp.where(pick, -jnp.inf, g)
    bias = jnp.where(jnp.logical_and(valid, keep > 0.5), 0.0, NEG)
    zeros = lambda rows: jnp.zeros((rows, g.shape[1]), F32)
    return jnp.concatenate([zeros(HEAD_DIM), bias, zeros(LANES - HEAD_DIM - n_gate)], axis=0).T


def _attn_kernel(q_ref, k_ref, v_ref, o_ref, m_ref, acc_ref, sa_ref, sb_ref, sc_ref, *gmat, moba):
    nh, tq = q_ref.shape[1], q_ref.shape[2]
    assert tq == 2 * MOBA_BLOCK
    qi = pl.program_id(1)
    first_own = 2 * qi
    last_blk = k_ref.shape[2] // MOBA_BLOCK - 1

    def values_t(blk, hd):
        return v_ref[0, blk, hd * VT_ROWS:(hd + 1) * VT_ROWS, :]

    def scores_into(dst_ref, blk, queries_ref, q0=0):
        start = pl.multiple_of(jnp.minimum(blk, last_blk) * MOBA_BLOCK, MOBA_BLOCK)
        for hd in range(nh):
            dst_ref[hd, :, 0:tq - q0] = lax.dot_general(
                k_ref[0, hd, pl.ds(start, MOBA_BLOCK), :], queries_ref[hd, q0:tq, :], _NT,
                preferred_element_type=F32)

    def consume(src_ref, vblk, mask=None, first=False, q0=0):
        def scores(hd):
            s = src_ref[hd, :, 0:tq - q0]
            return s if mask is None else jnp.where(mask, s, NEG)

        for hd in range(nh):
            smax = jnp.max(scores(hd), axis=0, keepdims=True)
            m_new = smax if first else jnp.maximum(m_ref[hd, :, q0:tq], smax)
            pv = jnp.dot(values_t(vblk, hd), jnp.exp2(scores(hd) - m_new).astype(BF16),
                         preferred_element_type=F32)
            acc_ref[hd, :, q0:tq] = pv if first else jnp.exp2(m_ref[hd, :, q0:tq] - m_new) * acc_ref[hd, :, q0:tq] + pv
            m_ref[hd, :, q0:tq] = m_new

    q_own_ref = q_ref.at[0]
    if moba:
        gmat_ref, q_past_ref, q_mix_ref = gmat

        @pl.when(qi == 0)
        def _():
            for hd in range(nh):
                _block_means(k_ref, hd, gmat_ref)

        query = lax.broadcasted_iota(jnp.int32, (1, tq), 1)
        own = first_own + jnp.where(query >= MOBA_BLOCK, 1, 0)
        for hd in range(nh):
            gate_t = lax.dot_general(gmat_ref[hd].astype(BF16), q_own_ref[hd], _NT, preferred_element_type=F32)
            q_past_ref[hd] = (q_own_ref[hd].astype(F32) + _select_bias_t(gate_t, own)).astype(BF16)
            q_mix_ref[hd, 0:MOBA_BLOCK, :] = q_own_ref[hd, 0:MOBA_BLOCK, :]
            q_mix_ref[hd, MOBA_BLOCK:tq, :] = q_past_ref[hd, MOBA_BLOCK:tq, :]
    else:
        q_past_ref = q_mix_ref = q_own_ref

    scores_into(sb_ref, first_own, q_mix_ref)
    scores_into(sc_ref, first_own + 1, q_own_ref, q0=MOBA_BLOCK)
    scores_into(sa_ref, 0, q_past_ref)
    def causal(queries):
        return (lax.broadcasted_iota(jnp.int32, (MOBA_BLOCK, queries), 0)
                <= lax.broadcasted_iota(jnp.int32, (MOBA_BLOCK, queries), 1))

    consume(sb_ref, first_own, mask=causal(tq), first=True)
    consume(sc_ref, first_own + 1, mask=causal(tq - MOBA_BLOCK), q0=MOBA_BLOCK)

    def block_pair(t):
        scores_into(sb_ref, 2 * t + 1, q_past_ref)
        consume(sa_ref, 2 * t)
        scores_into(sa_ref, 2 * t + 2, q_past_ref)
        consume(sb_ref, 2 * t + 1)

    def two_pairs(u, carry):
        block_pair(2 * u)
        block_pair(2 * u + 1)
        return carry

    lax.fori_loop(0, lax.shift_right_logical(qi, 1), two_pairs, 0)

    @pl.when(lax.rem(qi, 2) == 1)
    def _():
        block_pair(qi - 1)

    heads = []
    for hd in range(nh):
        acc = acc_ref[hd]
        heads.append(acc[:HEAD_DIM] * (1.0 / acc[HEAD_DIM:HEAD_DIM + 1]))
    o_ref[0] = jnp.concatenate(heads, axis=0).T.astype(BF16)


def _attention(q, k, vt, moba):
    b, nh, s, _ = q.shape
    tq = ATTN_TILE
    k_spec = pl.BlockSpec((1, nh, s, LANES), lambda bi, i: (bi, 0, 0, 0), pipeline_mode=pl.Buffered(1))
    v_spec = pl.BlockSpec((1,) + vt.shape[1:], lambda bi, i: (bi, 0, 0, 0), pipeline_mode=pl.Buffered(1))
    scores = pltpu.VMEM((nh, MOBA_BLOCK, tq), F32)
    queries = pltpu.VMEM((nh, tq, LANES), BF16)
    return pl.pallas_call(
        functools.partial(_attn_kernel, moba=moba),
        grid=(b, s // tq),
        in_specs=[pl.BlockSpec((1, nh, tq, LANES), lambda bi, i: (bi, 0, i, 0)), k_spec, v_spec],
        out_specs=pl.BlockSpec((1, tq, nh * HEAD_DIM), lambda bi, i: (bi, i, 0)),
        out_shape=jax.ShapeDtypeStruct((b, s, nh * HEAD_DIM), BF16),
        scratch_shapes=[pltpu.VMEM((nh, 1, tq), F32), pltpu.VMEM((nh, VT_ROWS, tq), F32), scores, scores, scores]
        + ([pltpu.VMEM((nh, LANES, LANES), F32), queries, queries] if moba else []),
        compiler_params=pltpu.CompilerParams(
            dimension_semantics=("arbitrary", "arbitrary"), vmem_limit_bytes=VMEM_LIMIT),
        name="moba_attn" if moba else "mla_attn",
    )(q, k, vt)


def _mix_out_kernel(x_ref, ya_ref, yb_ref, yc_ref, yd_ref, wo_ref, gpost_ref, gpre_ref, gfpost_ref,
                    wup_ref, cw_ref, wdn_ref, o_ref, ext_ref, carry_ref, act_ref):
    tm = x_ref.shape[1]
    first = pl.program_id(1) == 0
    mixed = jnp.concatenate([ya_ref[0], yb_ref[0], yc_ref[0], yd_ref[0]], axis=1)
    y = jnp.dot(mixed, wo_ref[...], preferred_element_type=F32)
    x1 = x_ref[0] + _rms(y, gpost_ref[...])
    hn = _rms(x1, gpre_ref[...]).astype(BF16)

    @pl.when(first)
    def _():
        carry_ref[...] = jnp.zeros(carry_ref.shape, F32)

    fc = FFN_CHUNK
    n_chunks = D_FF // fc

    def up_project(c):
        for part in range(2):
            c0 = part * D_FF + c * fc
            ext = ext_ref.at[c % 2, part]
            ext[0:SUBLANES, :] = carry_ref[:, c0:c0 + fc]
            ext[SUBLANES:SUBLANES + tm, :] = jnp.dot(hn, wup_ref[:, c0:c0 + fc], preferred_element_type=F32)
            carry_ref[:, c0:c0 + fc] = ext[tm:tm + SUBLANES, :]

    def conv3(c, part):
        c0 = part * D_FF + c * fc
        ext = ext_ref.at[c % 2, part]
        conv = cw_ref[0:1, c0:c0 + fc] * ext[SUBLANES - 2:SUBLANES - 2 + tm, :]
        conv += cw_ref[1:2, c0:c0 + fc] * ext[SUBLANES - 1:SUBLANES - 1 + tm, :]
        return conv + cw_ref[2:3, c0:c0 + fc] * ext[SUBLANES:SUBLANES + tm, :]

    up_project(0)
    f = None
    done = 0
    for c in range(n_chunks):
        if c + 1 < n_chunks:
            up_project(c + 1)
        gate = conv3(c, 0)
        act_ref[:, c * fc:(c + 1) * fc] = (gate * _sigmoid(gate) * conv3(c, 1)).astype(BF16)
        if (c + 1) % DOWN_CHUNKS == 0 or c + 1 == n_chunks:
            part = jnp.dot(act_ref[:, done * fc:(c + 1) * fc], wdn_ref[done * fc:(c + 1) * fc, :],
                           preferred_element_type=F32)
            f = part if f is None else f + part
            done = c + 1
    o_ref[0] = x1 + _rms(f, gfpost_ref[...])


def _mix_out(layer, x, ya, yb, yc, yd, wo, gpost, gpre, gfpost, wup, cw, wdn):
    b, s, _ = x.shape
    params = (wo, gpost, gpre, gfpost, wup, cw, wdn)
    tm = TOKEN_TILE
    x_spec = pl.BlockSpec((1, tm, D_MODEL), lambda bi, i: (bi, i, 0))
    group_spec = pl.BlockSpec((1, tm, GROUP_WIDTH), lambda bi, i: (bi, i, 0))
    return pl.pallas_call(
        _mix_out_kernel,
        grid=(b, s // tm),
        in_specs=[x_spec, group_spec, group_spec, group_spec, group_spec]
        + [_layer_spec(p, layer) for p in params],
        out_specs=x_spec,
        out_shape=jax.ShapeDtypeStruct(x.shape, F32),
        scratch_shapes=[pltpu.VMEM((2, 2, tm + SUBLANES, FFN_CHUNK), F32),
                        pltpu.VMEM((SUBLANES, 2 * D_FF), F32),
                        pltpu.VMEM((tm, D_FF), BF16)],
        compiler_params=pltpu.CompilerParams(
            dimension_semantics=("arbitrary", "arbitrary"), vmem_limit_bytes=VMEM_LIMIT),
        name="mix_out",
    )(x, ya, yb, yc, yd, wo, gpost, gpre, gfpost, wup, cw, wdn)


def _pad_heads(w, widths):
    per_head = sum(widths)
    w = w.reshape(w.shape[:-1] + (N_HEADS, per_head))
    w = jnp.pad(w, [(0, 0)] * (w.ndim - 1) + [(0, LANES - per_head)])
    return w.reshape(w.shape[:-2] + (N_HEADS * LANES,))


def _rope_tables(seq, dim, first_lane):
    half = dim // 2
    rel = np.arange(LANES) - first_lane
    rotary = (rel >= 0) & (rel < dim)
    inv = 1.0 / (ROPE_THETA ** (jnp.arange(0, dim, 2, dtype=F32) / dim))
    inv_lane = jnp.where(rotary, inv[np.where(rotary, rel % half, 0)], 0.0)
    ang = jnp.arange(seq, dtype=F32)[:, None] * inv_lane[None, :]
    sin = jnp.sin(ang)
    return jnp.stack([jnp.cos(ang),
                      jnp.where(rotary & (rel < half), -sin, 0.0),
                      jnp.where(rotary & (rel >= half), sin, 0.0)])


def _prep_w_in(w):
    sc, cf = w[..., :768], w[..., 768:1280]
    mq, mk, mv = w[..., 1280:1536], w[..., 1536:1792], w[..., 1792:2048]
    cq, ckv, kr = w[..., 2048:2432], w[..., 2432:2560], w[..., 2560:2592]
    kr = jnp.pad(kr, ((0, 0), (0, 0), (MLA_NOPE_DIM, LANES - MLA_NOPE_DIM - MLA_ROPE_DIM)))
    parts = [sc, cf, _pad_heads(mq, (HEAD_DIM,)), _pad_heads(mk, (HEAD_DIM,)), mv, cq, ckv, kr]
    return jnp.concatenate(parts, axis=-1).astype(BF16)


def _prep_w_ukv(w):
    w = w.reshape(w.shape[:-1] + (N_HEADS, 2, HEAD_DIM))
    k = _pad_heads(w[..., 0, :].reshape(w.shape[:2] + (-1,)), (HEAD_DIM,))
    v = w[..., 1, :].reshape(w.shape[:2] + (-1,))
    return jnp.concatenate([k, v], axis=-1).astype(BF16)


def kernel(x, norm_mix_pre, norm_mix_post, norm_ffn_pre, norm_ffn_post, w_in, conv_a, conv_b, conv_b_bias,
           ln_b_gain, ln_b_bias, mla_q_norm, mla_kv_norm, w_uq, w_ukv, w_out, w_ffn_up, conv_ffn, w_ffn_down):
    b, s, d = x.shape
    depth = w_in.shape[0]
    assert d == D_MODEL and s % TOKEN_TILE == 0 and s // MOBA_BLOCK <= MOBA_BLOCK // SUBLANES
    tc = _rope_tables(s, ROPE_DIM, 0)
    td = _rope_tables(s, MLA_ROPE_DIM, MLA_NOPE_DIM)
    blk = jnp.arange(s, dtype=jnp.int32)[:, None] // MOBA_BLOCK + HEAD_DIM
    oh = (blk == jnp.arange(LANES, dtype=jnp.int32)[None, :]).astype(F32)
    rows = lambda v: v.reshape(depth, 1, -1)
    in_params = (rows(norm_mix_pre), _prep_w_in(w_in), _pad_heads(w_uq, (MLA_NOPE_DIM, MLA_ROPE_DIM)).astype(BF16),
                 _prep_w_ukv(w_ukv), rows(mla_q_norm), rows(mla_kv_norm), conv_a, conv_b, rows(conv_b_bias),
                 rows(ln_b_gain), rows(ln_b_bias))
    out_params = (w_out.astype(BF16), rows(norm_mix_post), rows(norm_ffn_pre), rows(norm_ffn_post),
                  w_ffn_up.astype(BF16), conv_ffn, w_ffn_down.astype(BF16))
    for layer in range(depth):
        ya, yb, qc, kc, vc, qd, kd, vd = _mix_in(layer, x, *in_params, tc, td, oh)
        yc = _attention(qc, kc, vc, True)
        yd = _attention(qd, kd, vd, False)
        x = _mix_out(layer, x, ya, yb, yc, yd, *out_params)
    return x
```

```python
import functools

import jax
import jax.numpy as jnp
import numpy as np
from jax import lax
from jax.experimental import pallas as pl
from jax.experimental.pallas import tpu as pltpu

F32 = jnp.float32
BF16 = jnp.bfloat16

D_MODEL = 1024
GROUP_WIDTH = 256
N_HEADS = 4
HEAD_DIM = 64
SC_KERNEL = 3
CF_KERNEL = 31
FFN_KERNEL = 3
MOBA_BLOCK = 256
MOBA_TOPK = 3
ROPE_THETA = 500000.0
ROPE_DIM = 16
MLA_Q_RANK = 384
MLA_KV_RANK = 128
MLA_NOPE_DIM = 64
MLA_ROPE_DIM = 32
D_FF = 2816
EPS = 1e-6
NEG = -1e30
LOG2E = 1.4426950408889634

LANES = 128
SUBLANES = 8
VMEM_LIMIT = 56 * 1024 * 1024

TOKEN_TILE = 512
ATTN_TILE = 2 * MOBA_BLOCK
CONV_ROWS = 64
VT_ROWS = 80
CF_HALO = 32
FFN_CHUNK = 256
DOWN_CHUNKS = 6

_C_SC = 0
_C_CF = _C_SC + 3 * GROUP_WIDTH
_C_MQ = _C_CF + 2 * GROUP_WIDTH
_C_MK = _C_MQ + N_HEADS * LANES
_C_MV = _C_MK + N_HEADS * LANES
_C_CQ = _C_MV + GROUP_WIDTH
_C_CKV = _C_CQ + MLA_Q_RANK
_C_KR = _C_CKV + MLA_KV_RANK
_C_END = _C_KR + LANES


def _layer_spec(stacked, layer):
    zeros = (0,) * (stacked.ndim - 1)
    return pl.BlockSpec((None,) + stacked.shape[1:], lambda *_: (layer,) + zeros, pipeline_mode=pl.Buffered(1))


def _rms(x, g):
    return x * lax.rsqrt(jnp.mean(x * x, axis=-1, keepdims=True) + EPS) * g


def _sigmoid(x):
    return 1.0 / (1.0 + jnp.exp(-x))


def _rope(x, tab_ref, shift):
    return (x * tab_ref[0]
            + pltpu.roll(x, LANES - shift, 1) * tab_ref[1]
            + pltpu.roll(x, shift, 1) * tab_ref[2])


def _carry_halo(ext_ref, halo, rows, first):
    @pl.when(first)
    def _():
        ext_ref[0:halo, :] = jnp.zeros((halo, ext_ref.shape[1]), F32)

    @pl.when(jnp.logical_not(first))
    def _():
        ext_ref[0:halo, :] = ext_ref[rows:rows + halo, :]


def _store_values_t(vt_ref, v):
    ones = jnp.ones((VT_ROWS - HEAD_DIM, MOBA_BLOCK), BF16)
    for blk in range(v.shape[0] // MOBA_BLOCK):
        vt = v[blk * MOBA_BLOCK:(blk + 1) * MOBA_BLOCK, :].T.astype(BF16)
        for hd in range(N_HEADS):
            vt_ref[0, blk, hd * VT_ROWS:hd * VT_ROWS + HEAD_DIM, :] = vt[hd * HEAD_DIM:(hd + 1) * HEAD_DIM, :]
            vt_ref[0, blk, hd * VT_ROWS + HEAD_DIM:(hd + 1) * VT_ROWS, :] = ones


def _mix_in_kernel(x_ref, g_ref, w_ref, wuq_ref, wukv_ref, gq_ref, gkv_ref,
                   ca_ref, cb_ref, cbb_ref, lng_ref, lnb_ref, tc_ref, td_ref, oh_ref,
                   ya_ref, yb_ref, qc_ref, kc_ref, vc_ref, qd_ref, kd_ref, vd_ref,
                   exta_ref, extb_ref, shift_ref):
    tm = x_ref.shape[1]
    first = pl.program_id(1) == 0
    h = _rms(x_ref[0], g_ref[...]).astype(BF16)

    def proj(a, b):
        return jnp.dot(h, w_ref[:, a:b], preferred_element_type=F32)

    _carry_halo(exta_ref, SUBLANES, tm, first)
    _carry_halo(extb_ref, CF_HALO, tm, first)

    def short_conv():
        z = proj(_C_SC, _C_CF)
        exta_ref[SUBLANES:SUBLANES + tm, :] = z[:, 2 * GROUP_WIDTH:] * z[:, :GROUP_WIDTH]
        conv = ca_ref[0:1, :] * exta_ref[SUBLANES - 2:SUBLANES - 2 + tm, :]
        conv += ca_ref[1:2, :] * exta_ref[SUBLANES - 1:SUBLANES - 1 + tm, :]
        conv += ca_ref[2:3, :] * exta_ref[SUBLANES:SUBLANES + tm, :]
        ya_ref[0] = (z[:, GROUP_WIDTH:2 * GROUP_WIDTH] * conv).astype(BF16)

    z = proj(_C_CF, _C_MQ)
    extb_ref[CF_HALO:CF_HALO + tm, :] = z[:, :GROUP_WIDTH] * _sigmoid(z[:, GROUP_WIDTH:])
    span = tm + CF_HALO - SUBLANES
    for j in range(1, SUBLANES):
        shift_ref[j - 1, 0:span, :] = extb_ref[j:j + span, :]

    def conv_chunk(r):
        base = r * CONV_ROWS + CF_HALO - (CF_KERNEL - 1)
        acc = jnp.broadcast_to(cbb_ref[...], (CONV_ROWS, GROUP_WIDTH))
        for k in range(CF_KERNEL):
            offset = (base + k) % SUBLANES
            rows = extb_ref if offset == 0 else shift_ref.at[offset - 1]
            acc = acc + cb_ref[k:k + 1, :] * rows[base + k - offset:base + k - offset + CONV_ROWS, :]
        xc = acc - jnp.mean(acc, axis=-1, keepdims=True)
        y = xc * lax.rsqrt(jnp.mean(xc * xc, axis=-1, keepdims=True) + EPS) * lng_ref[...] + lnb_ref[...]
        yb_ref[0, r * CONV_ROWS:(r + 1) * CONV_ROWS, :] = (y * _sigmoid(y)).astype(BF16)

    scale_c = HEAD_DIM ** -0.5 * LOG2E

    def moba_q(pair):
        z = proj(_C_MQ + 2 * pair * LANES, _C_MQ + 2 * (pair + 1) * LANES)
        for j in range(2):
            qc_ref[0, 2 * pair + j] = (_rope(z[:, j * LANES:(j + 1) * LANES], tc_ref, ROPE_DIM // 2)
                                       * scale_c).astype(BF16)

    def moba_k(pair):
        z = proj(_C_MK + 2 * pair * LANES, _C_MK + 2 * (pair + 1) * LANES)
        for j in range(2):
            kc_ref[0, 2 * pair + j] = (_rope(z[:, j * LANES:(j + 1) * LANES], tc_ref, ROPE_DIM // 2)
                                       + oh_ref[...]).astype(BF16)

    def moba_v():
        _store_values_t(vc_ref, proj(_C_MV, _C_CQ))

    scale_d = (MLA_NOPE_DIM + MLA_ROPE_DIM) ** -0.5 * LOG2E

    def mla_q():
        cq = _rms(proj(_C_CQ, _C_CKV), gq_ref[...]).astype(BF16)
        qd = jnp.dot(cq, wuq_ref[...], preferred_element_type=F32)
        for hd in range(N_HEADS):
            qd_ref[0, hd] = (_rope(qd[:, hd * LANES:(hd + 1) * LANES], td_ref, MLA_ROPE_DIM // 2)
                             * scale_d).astype(BF16)

    def mla_kv():
        z = proj(_C_CKV, _C_END)
        ckv = _rms(z[:, :MLA_KV_RANK], gkv_ref[...]).astype(BF16)
        kr = _rope(z[:, MLA_KV_RANK:], td_ref, MLA_ROPE_DIM // 2)
        kvd = jnp.dot(ckv, wukv_ref[...], preferred_element_type=F32)
        for hd in range(N_HEADS):
            kd_ref[0, hd] = (kvd[:, hd * LANES:(hd + 1) * LANES] + kr).astype(BF16)
        _store_values_t(vd_ref, kvd[:, N_HEADS * LANES:])

    pieces = [short_conv, mla_q, mla_kv, lambda: moba_q(0), lambda: moba_q(1), lambda: moba_k(0),
              lambda: moba_k(1), moba_v]
    for r in range(tm // CONV_ROWS):
        conv_chunk(r)
        if r < len(pieces):
            pieces[r]()
    for piece in pieces[tm // CONV_ROWS:]:
        piece()


def _mix_in(layer, x, g, w, wuq, wukv, gq, gkv, ca, cb, cbb, lng, lnb, tc, td, oh):
    b, s, _ = x.shape
    params = (g, w, wuq, wukv, gq, gkv, ca, cb, cbb, lng, lnb)
    tm = TOKEN_TILE
    head_shape = jax.ShapeDtypeStruct((b, N_HEADS, s, LANES), BF16)
    head_spec = pl.BlockSpec((1, N_HEADS, tm, LANES), lambda bi, i: (bi, 0, i, 0))
    group_shape = jax.ShapeDtypeStruct((b, s, GROUP_WIDTH), BF16)
    group_spec = pl.BlockSpec((1, tm, GROUP_WIDTH), lambda bi, i: (bi, i, 0))
    tab_spec = pl.BlockSpec((3, tm, LANES), lambda bi, i: (0, i, 0))
    vt_shape = jax.ShapeDtypeStruct((b, s // MOBA_BLOCK, N_HEADS * VT_ROWS, MOBA_BLOCK), BF16)
    vt_spec = pl.BlockSpec((1, tm // MOBA_BLOCK, N_HEADS * VT_ROWS, MOBA_BLOCK), lambda bi, i: (bi, i, 0, 0))
    return pl.pallas_call(
        _mix_in_kernel,
        grid=(b, s // tm),
        in_specs=[pl.BlockSpec((1, tm, D_MODEL), lambda bi, i: (bi, i, 0))]
        + [_layer_spec(p, layer) for p in params]
        + [tab_spec, tab_spec, pl.BlockSpec((tm, LANES), lambda bi, i: (i, 0))],
        out_specs=[group_spec, group_spec] + [head_spec, head_spec, vt_spec] * 2,
        out_shape=[group_shape, group_shape] + [head_shape, head_shape, vt_shape] * 2,
        scratch_shapes=[pltpu.VMEM((tm + SUBLANES, GROUP_WIDTH), F32),
                        pltpu.VMEM((tm + CF_HALO, GROUP_WIDTH), F32),
                        pltpu.VMEM((SUBLANES - 1, tm + CF_HALO, GROUP_WIDTH), F32)],
        compiler_params=pltpu.CompilerParams(
            dimension_semantics=("arbitrary", "arbitrary"), vmem_limit_bytes=VMEM_LIMIT),
        name="mix_in",
    )(x, g, w, wuq, wukv, gq, gkv, ca, cb, cbb, lng, lnb, tc, td, oh)


_NT = (((1,), (1,)), ((), ()))


def _block_means(k_ref, hd, gmat_ref):
    nblk = k_ref.shape[2] // MOBA_BLOCK
    keep = (lax.broadcasted_iota(jnp.int32, (1, LANES), 1) < HEAD_DIM).astype(F32)
    gmat_ref[hd] = jnp.zeros((LANES, LANES), F32)
    for n in range(nblk):
        blk = k_ref[0, hd, n * MOBA_BLOCK:(n + 1) * MOBA_BLOCK, :].astype(F32)
        gmat_ref[hd, HEAD_DIM + n:HEAD_DIM + n + 1, :] = jnp.mean(blk, axis=0, keepdims=True) * keep


def _select_bias_t(gate_t, own):
    n_gate = MOBA_BLOCK // SUBLANES
    g = gate_t[HEAD_DIM:HEAD_DIM + n_gate, :]
    blk = lax.broadcasted_iota(jnp.int32, g.shape, 0)
    blk_f = blk.astype(F32)
    valid = blk < own
    g = jnp.where(valid, g, -jnp.inf)
    keep = jnp.zeros(g.shape, F32)
    for _ in range(MOBA_TOPK):
        top = jnp.max(g, axis=0, keepdims=True)
        first = jnp.min(jnp.where(g == top, blk_f, float(n_gate)), axis=0, keepdims=True)
        pick = blk_f == first
        keep = jnp.where(pick, 1.0, keep)
        g = jnp.where(pick, -jnp.inf, g)
    bias = jnp.where(jnp.logical_and(valid, keep > 0.5), 0.0, NEG)
    zeros = lambda rows: jnp.zeros((rows, g.shape[1]), F32)
    return jnp.concatenate([zeros(HEAD_DIM), bias, zeros(LANES - HEAD_DIM - n_gate)], axis=0).T


def _attn_kernel(q_ref, k_ref, v_ref, o_ref, m_ref, acc_ref, sa_ref, sb_ref, sc_ref, *gmat, moba):
    nh, tq = q_ref.shape[1], q_ref.shape[2]
    assert tq == 2 * MOBA_BLOCK
    qi = pl.program_id(1)
    first_own = 2 * qi
    last_blk = k_ref.shape[2] // MOBA_BLOCK - 1

    def values_t(blk, hd):
        return v_ref[0, blk, hd * VT_ROWS:(hd + 1) * VT_ROWS, :]

    def scores_into(dst_ref, blk, queries_ref, q0=0):
        start = pl.multiple_of(jnp.minimum(blk, last_blk) * MOBA_BLOCK, MOBA_BLOCK)
        for hd in range(nh):
            dst_ref[hd, :, 0:tq - q0] = lax.dot_general(
                k_ref[0, hd, pl.ds(start, MOBA_BLOCK), :], queries_ref[hd, q0:tq, :], _NT,
                preferred_element_type=F32)

    def consume(src_ref, vblk, mask=None, first=False, q0=0):
        def scores(hd):
            s = src_ref[hd, :, 0:tq - q0]
            return s if mask is None else jnp.where(mask, s, NEG)

        for hd in range(nh):
            smax = jnp.max(scores(hd), axis=0, keepdims=True)
            m_new = smax if first else jnp.maximum(m_ref[hd, :, q0:tq], smax)
            pv = jnp.dot(values_t(vblk, hd), jnp.exp2(scores(hd) - m_new).astype(BF16),
                         preferred_element_type=F32)
            acc_ref[hd, :, q0:tq] = pv if first else jnp.exp2(m_ref[hd, :, q0:tq] - m_new) * acc_ref[hd, :, q0:tq] + pv
            m_ref[hd, :, q0:tq] = m_new

    q_own_ref = q_ref.at[0]
    if moba:
        gmat_ref, q_past_ref, q_mix_ref = gmat

        @pl.when(qi == 0)
        def _():
            for hd in range(nh):
                _block_means(k_ref, hd, gmat_ref)

        query = lax.broadcasted_iota(jnp.int32, (1, tq), 1)
        own = first_own + jnp.where(query >= MOBA_BLOCK, 1, 0)
        for hd in range(nh):
            gate_t = lax.dot_general(gmat_ref[hd].astype(BF16), q_own_ref[hd], _NT, preferred_element_type=F32)
            q_past_ref[hd] = (q_own_ref[hd].astype(F32) + _select_bias_t(gate_t, own)).astype(BF16)
            q_mix_ref[hd, 0:MOBA_BLOCK, :] = q_own_ref[hd, 0:MOBA_BLOCK, :]
            q_mix_ref[hd, MOBA_BLOCK:tq, :] = q_past_ref[hd, MOBA_BLOCK:tq, :]
    else:
        q_past_ref = q_mix_ref = q_own_ref

    scores_into(sb_ref, first_own, q_mix_ref)
    scores_into(sc_ref, first_own + 1, q_own_ref, q0=MOBA_BLOCK)
    scores_into(sa_ref, 0, q_past_ref)
    def causal(queries):
        return (lax.broadcasted_iota(jnp.int32, (MOBA_BLOCK, queries), 0)
                <= lax.broadcasted_iota(jnp.int32, (MOBA_BLOCK, queries), 1))

    consume(sb_ref, first_own, mask=causal(tq), first=True)
    consume(sc_ref, first_own + 1, mask=causal(tq - MOBA_BLOCK), q0=MOBA_BLOCK)

    def block_pair(t):
        scores_into(sb_ref, 2 * t + 1, q_past_ref)
        consume(sa_ref, 2 * t)
        scores_into(sa_ref, 2 * t + 2, q_past_ref)
        consume(sb_ref, 2 * t + 1)

    def two_pairs(u, carry):
        block_pair(2 * u)
        block_pair(2 * u + 1)
        return carry

    lax.fori_loop(0, lax.shift_right_logical(qi, 1), two_pairs, 0)

    @pl.when(lax.rem(qi, 2) == 1)
    def _():
        block_pair(qi - 1)

    heads = []
    for hd in range(nh):
        acc = acc_ref[hd]
        heads.append(acc[:HEAD_DIM] * (1.0 / acc[HEAD_DIM:HEAD_DIM + 1]))
    o_ref[0] = jnp.concatenate(heads, axis=0).T.astype(BF16)


def _attention(q, k, vt, moba):
    b, nh, s, _ = q.shape
    tq = ATTN_TILE
    k_spec = pl.BlockSpec((1, nh, s, LANES), lambda bi, i: (bi, 0, 0, 0), pipeline_mode=pl.Buffered(1))
    v_spec = pl.BlockSpec((1,) + vt.shape[1:], lambda bi, i: (bi, 0, 0, 0), pipeline_mode=pl.Buffered(1))
    scores = pltpu.VMEM((nh, MOBA_BLOCK, tq), F32)
    queries = pltpu.VMEM((nh, tq, LANES), BF16)
    return pl.pallas_call(
        functools.partial(_attn_kernel, moba=moba),
        grid=(b, s // tq),
        in_specs=[pl.BlockSpec((1, nh, tq, LANES), lambda bi, i: (bi, 0, i, 0)), k_spec, v_spec],
        out_specs=pl.BlockSpec((1, tq, nh * HEAD_DIM), lambda bi, i: (bi, i, 0)),
        out_shape=jax.ShapeDtypeStruct((b, s, nh * HEAD_DIM), BF16),
        scratch_shapes=[pltpu.VMEM((nh, 1, tq), F32), pltpu.VMEM((nh, VT_ROWS, tq), F32), scores, scores, scores]
        + ([pltpu.VMEM((nh, LANES, LANES), F32), queries, queries] if moba else []),
        compiler_params=pltpu.CompilerParams(
            dimension_semantics=("arbitrary", "arbitrary"), vmem_limit_bytes=VMEM_LIMIT),
        name="moba_attn" if moba else "mla_attn",
    )(q, k, vt)


def _mix_out_kernel(x_ref, ya_ref, yb_ref, yc_ref, yd_ref, wo_ref, gpost_ref, gpre_ref, gfpost_ref,
                    wup_ref, cw_ref, wdn_ref, o_ref, ext_ref, carry_ref, act_ref):
    tm = x_ref.shape[1]
    first = pl.program_id(1) == 0
    mixed = jnp.concatenate([ya_ref[0], yb_ref[0], yc_ref[0], yd_ref[0]], axis=1)
    y = jnp.dot(mixed, wo_ref[...], preferred_element_type=F32)
    x1 = x_ref[0] + _rms(y, gpost_ref[...])
    hn = _rms(x1, gpre_ref[...]).astype(BF16)

    @pl.when(first)
    def _():
        carry_ref[...] = jnp.zeros(carry_ref.shape, F32)

    fc = FFN_CHUNK
    n_chunks = D_FF // fc

    def up_project(c):
        for part in range(2):
            c0 = part * D_FF + c * fc
            ext = ext_ref.at[c % 2, part]
            ext[0:SUBLANES, :] = carry_ref[:, c0:c0 + fc]
            ext[SUBLANES:SUBLANES + tm, :] = jnp.dot(hn, wup_ref[:, c0:c0 + fc], preferred_element_type=F32)
            carry_ref[:, c0:c0 + fc] = ext[tm:tm + SUBLANES, :]

    def conv3(c, part):
        c0 = part * D_FF + c * fc
        ext = ext_ref.at[c % 2, part]
        conv = cw_ref[0:1, c0:c0 + fc] * ext[SUBLANES - 2:SUBLANES - 2 + tm, :]
        conv += cw_ref[1:2, c0:c0 + fc] * ext[SUBLANES - 1:SUBLANES - 1 + tm, :]
        return conv + cw_ref[2:3, c0:c0 + fc] * ext[SUBLANES:SUBLANES + tm, :]

    up_project(0)
    f = None
    done = 0
    for c in range(n_chunks):
        if c + 1 < n_chunks:
            up_project(c + 1)
        gate = conv3(c, 0)
        act_ref[:, c * fc:(c + 1) * fc] = (gate * _sigmoid(gate) * conv3(c, 1)).astype(BF16)
        if (c + 1) % DOWN_CHUNKS == 0 or c + 1 == n_chunks:
            part = jnp.dot(act_ref[:, done * fc:(c + 1) * fc], wdn_ref[done * fc:(c + 1) * fc, :],
                           preferred_element_type=F32)
            f = part if f is None else f + part
            done = c + 1
    o_ref[0] = x1 + _rms(f, gfpost_ref[...])


def _mix_out(layer, x, ya, yb, yc, yd, wo, gpost, gpre, gfpost, wup, cw, wdn):
    b, s, _ = x.shape
    params = (wo, gpost, gpre, gfpost, wup, cw, wdn)
    tm = TOKEN_TILE
    x_spec = pl.BlockSpec((1, tm, D_MODEL), lambda bi, i: (bi, i, 0))
    group_spec = pl.BlockSpec((1, tm, GROUP_WIDTH), lambda bi, i: (bi, i, 0))
    return pl.pallas_call(
        _mix_out_kernel,
        grid=(b, s // tm),
        in_specs=[x_spec, group_spec, group_spec, group_spec, group_spec]
        + [_layer_spec(p, layer) for p in params],
        out_specs=x_spec,
        out_shape=jax.ShapeDtypeStruct(x.shape, F32),
        scratch_shapes=[pltpu.VMEM((2, 2, tm + SUBLANES, FFN_CHUNK), F32),
                        pltpu.VMEM((SUBLANES, 2 * D_FF), F32),
                        pltpu.VMEM((tm, D_FF), BF16)],
        compiler_params=pltpu.CompilerParams(
            dimension_semantics=("arbitrary", "arbitrary"), vmem_limit_bytes=VMEM_LIMIT),
        name="mix_out",
    )(x, ya, yb, yc, yd, wo, gpost, gpre, gfpost, wup, cw, wdn)


def _pad_heads(w, widths):
    per_head = sum(widths)
    w = w.reshape(w.shape[:-1] + (N_HEADS, per_head))
    w = jnp.pad(w, [(0, 0)] * (w.ndim - 1) + [(0, LANES - per_head)])
    return w.reshape(w.shape[:-2] + (N_HEADS * LANES,))


def _rope_tables(seq, dim, first_lane):
    half = dim // 2
    rel = np.arange(LANES) - first_lane
    rotary = (rel >= 0) & (rel < dim)
    inv = 1.0 / (ROPE_THETA ** (jnp.arange(0, dim, 2, dtype=F32) / dim))
    inv_lane = jnp.where(rotary, inv[np.where(rotary, rel % half, 0)], 0.0)
    ang = jnp.arange(seq, dtype=F32)[:, None] * inv_lane[None, :]
    sin = jnp.sin(ang)
    return jnp.stack([jnp.cos(ang),
                      jnp.where(rotary & (rel < half), -sin, 0.0),
                      jnp.where(rotary & (rel >= half), sin, 0.0)])


def _prep_w_in(w):
    sc, cf = w[..., :768], w[..., 768:1280]
    mq, mk, mv = w[..., 1280:1536], w[..., 1536:1792], w[..., 1792:2048]
    cq, ckv, kr = w[..., 2048:2432], w[..., 2432:2560], w[..., 2560:2592]
    kr = jnp.pad(kr, ((0, 0), (0, 0), (MLA_NOPE_DIM, LANES - MLA_NOPE_DIM - MLA_ROPE_DIM)))
    parts = [sc, cf, _pad_heads(mq, (HEAD_DIM,)), _pad_heads(mk, (HEAD_DIM,)), mv, cq, ckv, kr]
    return jnp.concatenate(parts, axis=-1).astype(BF16)


def _prep_w_ukv(w):
    w = w.reshape(w.shape[:-1] + (N_HEADS, 2, HEAD_DIM))
    k = _pad_heads(w[..., 0, :].reshape(w.shape[:2] + (-1,)), (HEAD_DIM,))
    v = w[..., 1, :].reshape(w.shape[:2] + (-1,))
    return jnp.concatenate([k, v], axis=-1).astype(BF16)


def kernel(x, norm_mix_pre, norm_mix_post, norm_ffn_pre, norm_ffn_post, w_in, conv_a, conv_b, conv_b_bias,
           ln_b_gain, ln_b_bias, mla_q_norm, mla_kv_norm, w_uq, w_ukv, w_out, w_ffn_up, conv_ffn, w_ffn_down):
    b, s, d = x.shape
    depth = w_in.shape[0]
    assert d == D_MODEL and s % TOKEN_TILE == 0 and s // MOBA_BLOCK <= MOBA_BLOCK // SUBLANES
    tc = _rope_tables(s, ROPE_DIM, 0)
    td = _rope_tables(s, MLA_ROPE_DIM, MLA_NOPE_DIM)
    blk = jnp.arange(s, dtype=jnp.int32)[:, None] // MOBA_BLOCK + HEAD_DIM
    oh = (blk == jnp.arange(LANES, dtype=jnp.int32)[None, :]).astype(F32)
    rows = lambda v: v.reshape(depth, 1, -1)
    in_params = (rows(norm_mix_pre), _prep_w_in(w_in), _pad_heads(w_uq, (MLA_NOPE_DIM, MLA_ROPE_DIM)).astype(BF16),
                 _prep_w_ukv(w_ukv), rows(mla_q_norm), rows(mla_kv_norm), conv_a, conv_b, rows(conv_b_bias),
                 rows(ln_b_gain), rows(ln_b_bias))
    out_params = (w_out.astype(BF16), rows(norm_mix_post), rows(norm_ffn_pre), rows(norm_ffn_post),
                  w_ffn_up.astype(BF16), conv_ffn, w_ffn_down.astype(BF16))
    for layer in range(depth):
        ya, yb, qc, kc, vc, qd, kd, vd = _mix_in(layer, x, *in_params, tc, td, oh)
        yc = _attention(qc, kc, vc, True)
        yd = _attention(qd, kd, vd, False)
        x = _mix_out(layer, x, ya, yb, yc, yd, *out_params)
    return x
```

```python
import functools

import jax
import jax.numpy as jnp
import numpy as np
from jax import lax
from jax.experimental import pallas as pl
from jax.experimental.pallas import tpu as pltpu

F32 = jnp.float32
BF16 = jnp.bfloat16

D_MODEL = 1024
GROUP_WIDTH = 256
N_HEADS = 4
HEAD_DIM = 64
SC_KERNEL = 3
CF_KERNEL = 31
FFN_KERNEL = 3
MOBA_BLOCK = 256
MOBA_TOPK = 3
ROPE_THETA = 500000.0
ROPE_DIM = 16
MLA_Q_RANK = 384
MLA_KV_RANK = 128
MLA_NOPE_DIM = 64
MLA_ROPE_DIM = 32
D_FF = 2816
EPS = 1e-6
NEG = -1e30
LOG2E = 1.4426950408889634

LANES = 128
SUBLANES = 8
VMEM_LIMIT = 56 * 1024 * 1024

TOKEN_TILE = 512
ATTN_TILE = 2 * MOBA_BLOCK
CONV_ROWS = 64
VT_ROWS = 80
CF_HALO = 32
FFN_CHUNK = 256
DOWN_CHUNKS = 6

_C_SC = 0
_C_CF = _C_SC + 3 * GROUP_WIDTH
_C_MQ = _C_CF + 2 * GROUP_WIDTH
_C_MK = _C_MQ + N_HEADS * LANES
_C_MV = _C_MK + N_HEADS * LANES
_C_CQ = _C_MV + GROUP_WIDTH
_C_CKV = _C_CQ + MLA_Q_RANK
_C_KR = _C_CKV + MLA_KV_RANK
_C_END = _C_KR + LANES


def _layer_spec(stacked, layer):
    zeros = (0,) * (stacked.ndim - 1)
    return pl.BlockSpec((None,) + stacked.shape[1:], lambda *_: (layer,) + zeros, pipeline_mode=pl.Buffered(1))


def _rms(x, g):
    return x * lax.rsqrt(jnp.mean(x * x, axis=-1, keepdims=True) + EPS) * g


def _sigmoid(x):
    return 1.0 / (1.0 + jnp.exp(-x))


def _rope(x, tab_ref, shift):
    return (x * tab_ref[0]
            + pltpu.roll(x, LANES - shift, 1) * tab_ref[1]
            + pltpu.roll(x, shift, 1) * tab_ref[2])


def _carry_halo(ext_ref, halo, rows, first):
    @pl.when(first)
    def _():
        ext_ref[0:halo, :] = jnp.zeros((halo, ext_ref.shape[1]), F32)

    @pl.when(jnp.logical_not(first))
    def _():
        ext_ref[0:halo, :] = ext_ref[rows:rows + halo, :]


def _store_values_t(vt_ref, v):
    ones = jnp.ones((VT_ROWS - HEAD_DIM, MOBA_BLOCK), BF16)
    for blk in range(v.shape[0] // MOBA_BLOCK):
        vt = v[blk * MOBA_BLOCK:(blk + 1) * MOBA_BLOCK, :].T.astype(BF16)
        for hd in range(N_HEADS):
            vt_ref[0, blk, hd * VT_ROWS:hd * VT_ROWS + HEAD_DIM, :] = vt[hd * HEAD_DIM:(hd + 1) * HEAD_DIM, :]
            vt_ref[0, blk, hd * VT_ROWS + HEAD_DIM:(hd + 1) * VT_ROWS, :] = ones


def _mix_in_kernel(x_ref, g_ref, w_ref, wuq_ref, wukv_ref, gq_ref, gkv_ref,
                   ca_ref, cb_ref, cbb_ref, lng_ref, lnb_ref, tc_ref, td_ref, oh_ref,
                   ya_ref, yb_ref, qc_ref, kc_ref, vc_ref, qd_ref, kd_ref, vd_ref,
                   exta_ref, extb_ref, shift_ref):
    tm = x_ref.shape[1]
    first = pl.program_id(1) == 0
    h = _rms(x_ref[0], g_ref[...]).astype(BF16)

    def proj(a, b):
        return jnp.dot(h, w_ref[:, a:b], preferred_element_type=F32)

    _carry_halo(exta_ref, SUBLANES, tm, first)
    _carry_halo(extb_ref, CF_HALO, tm, first)

    def short_conv():
        z = proj(_C_SC, _C_CF)
        exta_ref[SUBLANES:SUBLANES + tm, :] = z[:, 2 * GROUP_WIDTH:] * z[:, :GROUP_WIDTH]
        conv = ca_ref[0:1, :] * exta_ref[SUBLANES - 2:SUBLANES - 2 + tm, :]
        conv += ca_ref[1:2, :] * exta_ref[SUBLANES - 1:SUBLANES - 1 + tm, :]
        conv += ca_ref[2:3, :] * exta_ref[SUBLANES:SUBLANES + tm, :]
        ya_ref[0] = (z[:, GROUP_WIDTH:2 * GROUP_WIDTH] * conv).astype(BF16)

    z = proj(_C_CF, _C_MQ)
    extb_ref[CF_HALO:CF_HALO + tm, :] = z[:, :GROUP_WIDTH] * _sigmoid(z[:, GROUP_WIDTH:])
    span = tm + CF_HALO - SUBLANES
    for j in range(1, SUBLANES):
        shift_ref[j - 1, 0:span, :] = extb_ref[j:j + span, :]

    def conv_chunk(r):
        base = r * CONV_ROWS + CF_HALO - (CF_KERNEL - 1)
        acc = jnp.broadcast_to(cbb_ref[...], (CONV_ROWS, GROUP_WIDTH))
        for k in range(CF_KERNEL):
            offset = (base + k) % SUBLANES
            rows = extb_ref if offset == 0 else shift_ref.at[offset - 1]
            acc = acc + cb_ref[k:k + 1, :] * rows[base + k - offset:base + k - offset + CONV_ROWS, :]
        xc = acc - jnp.mean(acc, axis=-1, keepdims=True)
        y = xc * lax.rsqrt(jnp.mean(xc * xc, axis=-1, keepdims=True) + EPS) * lng_ref[...] + lnb_ref[...]
        yb_ref[0, r * CONV_ROWS:(r + 1) * CONV_ROWS, :] = (y * _sigmoid(y)).astype(BF16)

    scale_c = HEAD_DIM ** -0.5 * LOG2E

    def moba_q(pair):
        z = proj(_C_MQ + 2 * pair * LANES, _C_MQ + 2 * (pair + 1) * LANES)
        for j in range(2):
            qc_ref[0, 2 * pair + j] = (_rope(z[:, j * LANES:(j + 1) * LANES], tc_ref, ROPE_DIM // 2)
                                       * scale_c).astype(BF16)

    def moba_k(pair):
        z = proj(_C_MK + 2 * pair * LANES, _C_MK + 2 * (pair + 1) * LANES)
        for j in range(2):
            kc_ref[0, 2 * pair + j] = (_rope(z[:, j * LANES:(j + 1) * LANES], tc_ref, ROPE_DIM // 2)
                                       + oh_ref[...]).astype(BF16)

    def moba_v():
        _store_values_t(vc_ref, proj(_C_MV, _C_CQ))

    scale_d = (MLA_NOPE_DIM + MLA_ROPE_DIM) ** -0.5 * LOG2E

    def mla_q():
        cq = _rms(proj(_C_CQ, _C_CKV), gq_ref[...]).astype(BF16)
        qd = jnp.dot(cq, wuq_ref[...], preferred_element_type=F32)
        for hd in range(N_HEADS):
            qd_ref[0, hd] = (_rope(qd[:, hd * LANES:(hd + 1) * LANES], td_ref, MLA_ROPE_DIM // 2)
                             * scale_d).astype(BF16)

    def mla_kv():
        z = proj(_C_CKV, _C_END)
        ckv = _rms(z[:, :MLA_KV_RANK], gkv_ref[...]).astype(BF16)
        kr = _rope(z[:, MLA_KV_RANK:], td_ref, MLA_ROPE_DIM // 2)
        kvd = jnp.dot(ckv, wukv_ref[...], preferred_element_type=F32)
        for hd in range(N_HEADS):
            kd_ref[0, hd] = (kvd[:, hd * LANES:(hd + 1) * LANES] + kr).astype(BF16)
        _store_values_t(vd_ref, kvd[:, N_HEADS * LANES:])

    pieces = [short_conv, mla_q, mla_kv, lambda: moba_q(0), lambda: moba_q(1), lambda: moba_k(0),
              lambda: moba_k(1), moba_v]
    for r in range(tm // CONV_ROWS):
        conv_chunk(r)
        if r < len(pieces):
            pieces[r]()
    for piece in pieces[tm // CONV_ROWS:]:
        piece()


def _mix_in(layer, x, g, w, wuq, wukv, gq, gkv, ca, cb, cbb, lng, lnb, tc, td, oh):
    b, s, _ = x.shape
    params = (g, w, wuq, wukv, gq, gkv, ca, cb, cbb, lng, lnb)
    tm = TOKEN_TILE
    head_shape = jax.ShapeDtypeStruct((b, N_HEADS, s, LANES), BF16)
    head_spec = pl.BlockSpec((1, N_HEADS, tm, LANES), lambda bi, i: (bi, 0, i, 0))
    group_shape = jax.ShapeDtypeStruct((b, s, GROUP_WIDTH), BF16)
    group_spec = pl.BlockSpec((1, tm, GROUP_WIDTH), lambda bi, i: (bi, i, 0))
    tab_spec = pl.BlockSpec((3, tm, LANES), lambda bi, i: (0, i, 0))
    vt_shape = jax.ShapeDtypeStruct((b, s // MOBA_BLOCK, N_HEADS * VT_ROWS, MOBA_BLOCK), BF16)
    vt_spec = pl.BlockSpec((1, tm // MOBA_BLOCK, N_HEADS * VT_ROWS, MOBA_BLOCK), lambda bi, i: (bi, i, 0, 0))
    return pl.pallas_call(
        _mix_in_kernel,
        grid=(b, s // tm),
        in_specs=[pl.BlockSpec((1, tm, D_MODEL), lambda bi, i: (bi, i, 0))]
        + [_layer_spec(p, layer) for p in params]
        + [tab_spec, tab_spec, pl.BlockSpec((tm, LANES), lambda bi, i: (i, 0))],
        out_specs=[group_spec, group_spec] + [head_spec, head_spec, vt_spec] * 2,
        out_shape=[group_shape, group_shape] + [head_shape, head_shape, vt_shape] * 2,
        scratch_shapes=[pltpu.VMEM((tm + SUBLANES, GROUP_WIDTH), F32),
                        pltpu.VMEM((tm + CF_HALO, GROUP_WIDTH), F32),
                        pltpu.VMEM((SUBLANES - 1, tm + CF_HALO, GROUP_WIDTH), F32)],
        compiler_params=pltpu.CompilerParams(
            dimension_semantics=("arbitrary", "arbitrary"), vmem_limit_bytes=VMEM_LIMIT),
        name="mix_in",
    )(x, g, w, wuq, wukv, gq, gkv, ca, cb, cbb, lng, lnb, tc, td, oh)


_NT = (((1,), (1,)), ((), ()))


def _block_means(k_ref, hd, gmat_ref):
    nblk = k_ref.shape[2] // MOBA_BLOCK
    keep = (lax.broadcasted_iota(jnp.int32, (1, LANES), 1) < HEAD_DIM).astype(F32)
    gmat_ref[hd] = jnp.zeros((LANES, LANES), F32)
    for n in range(nblk):
        blk = k_ref[0, hd, n * MOBA_BLOCK:(n + 1) * MOBA_BLOCK, :].astype(F32)
        gmat_ref[hd, HEAD_DIM + n:HEAD_DIM + n + 1, :] = jnp.mean(blk, axis=0, keepdims=True) * keep


def _select_bias_t(gate_t, own):
    n_gate = MOBA_BLOCK // SUBLANES
    g = gate_t[HEAD_DIM:HEAD_DIM + n_gate, :]
    blk = lax.broadcasted_iota(jnp.int32, g.shape, 0)
    blk_f = blk.astype(F32)
    valid = blk < own
    g = jnp.where(valid, g, -jnp.inf)
    keep = jnp.zeros(g.shape, F32)
    for _ in range(MOBA_TOPK):
        top = jnp.max(g, axis=0, keepdims=True)
        first = jnp.min(jnp.where(g == top, blk_f, float(n_gate)), axis=0, keepdims=True)
        pick = blk_f == first
        keep = jnp.where(pick, 1.0, keep)
        g = jnp.where(pick, -jnp.inf, g)
    bias = jnp.where(jnp.logical_and(valid, keep > 0.5), 0.0, NEG)
    zeros = lambda rows: jnp.zeros((rows, g.shape[1]), F32)
    return jnp.concatenate([zeros(HEAD_DIM), bias, zeros(LANES - HEAD_DIM - n_gate)], axis=0).T


def _attn_kernel(q_ref, k_ref, v_ref, o_ref, m_ref, acc_ref, sa_ref, sb_ref, sc_ref, *gmat, moba):
    nh, tq = q_ref.shape[1], q_ref.shape[2]
    assert tq == 2 * MOBA_BLOCK
    qi = pl.program_id(1)
    first_own = 2 * qi
    last_blk = k_ref.shape[2] // MOBA_BLOCK - 1

    def values_t(blk, hd):
        return v_ref[0, blk, hd * VT_ROWS:(hd + 1) * VT_ROWS, :]

    def scores_into(dst_ref, blk, queries_ref, q0=0, heads=None):
        start = pl.multiple_of(jnp.minimum(blk, last_blk) * MOBA_BLOCK, MOBA_BLOCK)
        for hd in (range(nh) if heads is None else heads):
            dst_ref[hd, :, 0:tq - q0] = lax.dot_general(
                k_ref[0, hd, pl.ds(start, MOBA_BLOCK), :], queries_ref[hd, q0:tq, :], _NT,
                preferred_element_type=F32)

    def consume(src_ref, vblk, mask=None, first=False, q0=0, heads=None):
        def scores(hd):
            s = src_ref[hd, :, 0:tq - q0]
            return s if mask is None else jnp.where(mask, s, NEG)

        for hd in (range(nh) if heads is None else heads):
            smax = jnp.max(scores(hd), axis=0, keepdims=True)
            m_new = smax if first else jnp.maximum(m_ref[hd, :, q0:tq], smax)
            pv = jnp.dot(values_t(vblk, hd), jnp.exp2(scores(hd) - m_new).astype(BF16),
                         preferred_element_type=F32)
            acc_ref[hd, :, q0:tq] = pv if first else jnp.exp2(m_ref[hd, :, q0:tq] - m_new) * acc_ref[hd, :, q0:tq] + pv
            m_ref[hd, :, q0:tq] = m_new

    q_own_ref = q_ref.at[0]
    if moba:
        gmat_ref, q_past_ref, q_mix_ref = gmat

        @pl.when(qi == 0)
        def _():
            for hd in range(nh):
                _block_means(k_ref, hd, gmat_ref)

        query = lax.broadcasted_iota(jnp.int32, (1, tq), 1)
        own = first_own + jnp.where(query >= MOBA_BLOCK, 1, 0)
        for hd in range(nh):
            gate_t = lax.dot_general(gmat_ref[hd].astype(BF16), q_own_ref[hd], _NT, preferred_element_type=F32)
            q_past_ref[hd] = (q_own_ref[hd].astype(F32) + _select_bias_t(gate_t, own)).astype(BF16)
            q_mix_ref[hd, 0:MOBA_BLOCK, :] = q_own_ref[hd, 0:MOBA_BLOCK, :]
            q_mix_ref[hd, MOBA_BLOCK:tq, :] = q_past_ref[hd, MOBA_BLOCK:tq, :]
    else:
        q_past_ref = q_mix_ref = q_own_ref

    scores_into(sb_ref, first_own, q_mix_ref)
    scores_into(sc_ref, first_own + 1, q_own_ref, q0=MOBA_BLOCK)
    scores_into(sa_ref, 0, q_past_ref)
    def causal(queries):
        return (lax.broadcasted_iota(jnp.int32, (MOBA_BLOCK, queries), 0)
                <= lax.broadcasted_iota(jnp.int32, (MOBA_BLOCK, queries), 1))

    consume(sb_ref, first_own, mask=causal(tq), first=True)
    consume(sc_ref, first_own + 1, mask=causal(tq - MOBA_BLOCK), q0=MOBA_BLOCK)

    def block_pair(t):
        for hd in range(nh):
            scores_into(sb_ref, 2 * t + 1, q_past_ref, heads=(hd,))
            consume(sa_ref, 2 * t, heads=(hd,))
        for hd in range(nh):
            scores_into(sa_ref, 2 * t + 2, q_past_ref, heads=(hd,))
            consume(sb_ref, 2 * t + 1, heads=(hd,))

    def two_pairs(u, carry):
        block_pair(2 * u)
        block_pair(2 * u + 1)
        return carry

    lax.fori_loop(0, lax.shift_right_logical(qi, 1), two_pairs, 0)

    @pl.when(lax.rem(qi, 2) == 1)
    def _():
        block_pair(qi - 1)

    heads = []
    for hd in range(nh):
        acc = acc_ref[hd]
        heads.append(acc[:HEAD_DIM] * (1.0 / acc[HEAD_DIM:HEAD_DIM + 1]))
    o_ref[0] = jnp.concatenate(heads, axis=0).T.astype(BF16)


def _attention(q, k, vt, moba):
    b, nh, s, _ = q.shape
    tq = ATTN_TILE
    k_spec = pl.BlockSpec((1, nh, s, LANES), lambda bi, i: (bi, 0, 0, 0), pipeline_mode=pl.Buffered(1))
    v_spec = pl.BlockSpec((1,) + vt.shape[1:], lambda bi, i: (bi, 0, 0, 0), pipeline_mode=pl.Buffered(1))
    scores = pltpu.VMEM((nh, MOBA_BLOCK, tq), F32)
    queries = pltpu.VMEM((nh, tq, LANES), BF16)
    return pl.pallas_call(
        functools.partial(_attn_kernel, moba=moba),
        grid=(b, s // tq),
        in_specs=[pl.BlockSpec((1, nh, tq, LANES), lambda bi, i: (bi, 0, i, 0)), k_spec, v_spec],
        out_specs=pl.BlockSpec((1, tq, nh * HEAD_DIM), lambda bi, i: (bi, i, 0)),
        out_shape=jax.ShapeDtypeStruct((b, s, nh * HEAD_DIM), BF16),
        scratch_shapes=[pltpu.VMEM((nh, 1, tq), F32), pltpu.VMEM((nh, VT_ROWS, tq), F32), scores, scores, scores]
        + ([pltpu.VMEM((nh, LANES, LANES), F32), queries, queries] if moba else []),
        compiler_params=pltpu.CompilerParams(
            dimension_semantics=("arbitrary", "arbitrary"), vmem_limit_bytes=VMEM_LIMIT),
        name="moba_attn" if moba else "mla_attn",
    )(q, k, vt)


def _mix_out_kernel(x_ref, ya_ref, yb_ref, yc_ref, yd_ref, wo_ref, gpost_ref, gpre_ref, gfpost_ref,
                    wup_ref, cw_ref, wdn_ref, o_ref, ext_ref, carry_ref, act_ref):
    tm = x_ref.shape[1]
    first = pl.program_id(1) == 0
    mixed = jnp.concatenate([ya_ref[0], yb_ref[0], yc_ref[0], yd_ref[0]], axis=1)
    y = jnp.dot(mixed, wo_ref[...], preferred_element_type=F32)
    x1 = x_ref[0] + _rms(y, gpost_ref[...])
    hn = _rms(x1, gpre_ref[...]).astype(BF16)

    @pl.when(first)
    def _():
        carry_ref[...] = jnp.zeros(carry_ref.shape, F32)

    fc = FFN_CHUNK
    n_chunks = D_FF // fc

    def up_project(c):
        for part in range(2):
            c0 = part * D_FF + c * fc
            ext = ext_ref.at[c % 2, part]
            ext[0:SUBLANES, :] = carry_ref[:, c0:c0 + fc]
            ext[SUBLANES:SUBLANES + tm, :] = jnp.dot(hn, wup_ref[:, c0:c0 + fc], preferred_element_type=F32)
            carry_ref[:, c0:c0 + fc] = ext[tm:tm + SUBLANES, :]

    def conv3(c, part):
        c0 = part * D_FF + c * fc
        ext = ext_ref.at[c % 2, part]
        conv = cw_ref[0:1, c0:c0 + fc] * ext[SUBLANES - 2:SUBLANES - 2 + tm, :]
        conv += cw_ref[1:2, c0:c0 + fc] * ext[SUBLANES - 1:SUBLANES - 1 + tm, :]
        return conv + cw_ref[2:3, c0:c0 + fc] * ext[SUBLANES:SUBLANES + tm, :]

    up_project(0)
    f = None
    done = 0
    for c in range(n_chunks):
        if c + 1 < n_chunks:
            up_project(c + 1)
        gate = conv3(c, 0)
        act_ref[:, c * fc:(c + 1) * fc] = (gate * _sigmoid(gate) * conv3(c, 1)).astype(BF16)
        if (c + 1) % DOWN_CHUNKS == 0 or c + 1 == n_chunks:
            part = jnp.dot(act_ref[:, done * fc:(c + 1) * fc], wdn_ref[done * fc:(c + 1) * fc, :],
                           preferred_element_type=F32)
            f = part if f is None else f + part
            done = c + 1
    o_ref[0] = x1 + _rms(f, gfpost_ref[...])


def _mix_out(layer, x, ya, yb, yc, yd, wo, gpost, gpre, gfpost, wup, cw, wdn):
    b, s, _ = x.shape
    params = (wo, gpost, gpre, gfpost, wup, cw, wdn)
    tm = TOKEN_TILE
    x_spec = pl.BlockSpec((1, tm, D_MODEL), lambda bi, i: (bi, i, 0))
    group_spec = pl.BlockSpec((1, tm, GROUP_WIDTH), lambda bi, i: (bi, i, 0))
    return pl.pallas_call(
        _mix_out_kernel,
        grid=(b, s // tm),
        in_specs=[x_spec, group_spec, group_spec, group_spec, group_spec]
        + [_layer_spec(p, layer) for p in params],
        out_specs=x_spec,
        out_shape=jax.ShapeDtypeStruct(x.shape, F32),
        scratch_shapes=[pltpu.VMEM((2, 2, tm + SUBLANES, FFN_CHUNK), F32),
                        pltpu.VMEM((SUBLANES, 2 * D_FF), F32),
                        pltpu.VMEM((tm, D_FF), BF16)],
        compiler_params=pltpu.CompilerParams(
            dimension_semantics=("arbitrary", "arbitrary"), vmem_limit_bytes=VMEM_LIMIT),
        name="mix_out",
    )(x, ya, yb, yc, yd, wo, gpost, gpre, gfpost, wup, cw, wdn)


def _pad_heads(w, widths):
    per_head = sum(widths)
    w = w.reshape(w.shape[:-1] + (N_HEADS, per_head))
    w = jnp.pad(w, [(0, 0)] * (w.ndim - 1) + [(0, LANES - per_head)])
    return w.reshape(w.shape[:-2] + (N_HEADS * LANES,))


def _rope_tables(seq, dim, first_lane):
    half = dim // 2
    rel = np.arange(LANES) - first_lane
    rotary = (rel >= 0) & (rel < dim)
    inv = 1.0 / (ROPE_THETA ** (jnp.arange(0, dim, 2, dtype=F32) / dim))
    inv_lane = jnp.where(rotary, inv[np.where(rotary, rel % half, 0)], 0.0)
    ang = jnp.arange(seq, dtype=F32)[:, None] * inv_lane[None, :]
    sin = jnp.sin(ang)
    return jnp.stack([jnp.cos(ang),
                      jnp.where(rotary & (rel < half), -sin, 0.0),
                      jnp.where(rotary & (rel >= half), sin, 0.0)])


def _prep_w_in(w):
    sc, cf = w[..., :768], w[..., 768:1280]
    mq, mk, mv = w[..., 1280:1536], w[..., 1536:1792], w[..., 1792:2048]
    cq, ckv, kr = w[..., 2048:2432], w[..., 2432:2560], w[..., 2560:2592]
    kr = jnp.pad(kr, ((0, 0), (0, 0), (MLA_NOPE_DIM, LANES - MLA_NOPE_DIM - MLA_ROPE_DIM)))
    parts = [sc, cf, _pad_heads(mq, (HEAD_DIM,)), _pad_heads(mk, (HEAD_DIM,)), mv, cq, ckv, kr]
    return jnp.concatenate(parts, axis=-1).astype(BF16)


def _prep_w_ukv(w):
    w = w.reshape(w.shape[:-1] + (N_HEADS, 2, HEAD_DIM))
    k = _pad_heads(w[..., 0, :].reshape(w.shape[:2] + (-1,)), (HEAD_DIM,))
    v = w[..., 1, :].reshape(w.shape[:2] + (-1,))
    return jnp.concatenate([k, v], axis=-1).astype(BF16)


def kernel(x, norm_mix_pre, norm_mix_post, norm_ffn_pre, norm_ffn_post, w_in, conv_a, conv_b, conv_b_bias,
           ln_b_gain, ln_b_bias, mla_q_norm, mla_kv_norm, w_uq, w_ukv, w_out, w_ffn_up, conv_ffn, w_ffn_down):
    b, s, d = x.shape
    depth = w_in.shape[0]
    assert d == D_MODEL and s % TOKEN_TILE == 0 and s // MOBA_BLOCK <= MOBA_BLOCK // SUBLANES
    tc = _rope_tables(s, ROPE_DIM, 0)
    td = _rope_tables(s, MLA_ROPE_DIM, MLA_NOPE_DIM)
    blk = jnp.arange(s, dtype=jnp.int32)[:, None] // MOBA_BLOCK + HEAD_DIM
    oh = (blk == jnp.arange(LANES, dtype=jnp.int32)[None, :]).astype(F32)
    rows = lambda v: v.reshape(depth, 1, -1)
    in_params = (rows(norm_mix_pre), _prep_w_in(w_in), _pad_heads(w_uq, (MLA_NOPE_DIM, MLA_ROPE_DIM)).astype(BF16),
                 _prep_w_ukv(w_ukv), rows(mla_q_norm), rows(mla_kv_norm), conv_a, conv_b, rows(conv_b_bias),
                 rows(ln_b_gain), rows(ln_b_bias))
    out_params = (w_out.astype(BF16), rows(norm_mix_post), rows(norm_ffn_pre), rows(norm_ffn_post),
                  w_ffn_up.astype(BF16), conv_ffn, w_ffn_down.astype(BF16))
    for layer in range(depth):
        ya, yb, qc, kc, vc, qd, kd, vd = _mix_in(layer, x, *in_params, tc, td, oh)
        yc = _attention(qc, kc, vc, True)
        yd = _attention(qd, kd, vd, False)
        x = _mix_out(layer, x, ya, yb, yc, yd, *out_params)
    return x
```
